```python
import math
import jax
import jax.numpy as jnp
from jax import lax
import numpy as np

D_MODEL = 1024
BATCH = 8
SEQ = 2048
DEPTH = 4

GRID_W = 64
CTX_LEN = 256
EPS = 1e-6
N_MOD = 6

ATT_HQ = 8
ATT_HKV = 2
ATT_GROUP = ATT_HQ // ATT_HKV
ATT_DH = 64
Q_BLOCK = 128
ROPE_THETA = 10000.0
ROPE_AXIS_PAIRS = ATT_DH // 4

DN_H = 4
DN_DK = 128
DN_DV = 128
DN_CONV = 4
DN_CHUNK = 64

CONF_CH = 512
CONF_KW = 31

N_EXPERTS = 32
TOP_K = 4
D_FF = 1024
SWIGLU_LIMIT = 7.0
SWIGLU_ALPHA = 1.702

N_BRANCH = 3
IN_SPLITS = (ATT_HQ * ATT_DH, ATT_HKV * ATT_DH, ATT_HKV * ATT_DH,
             DN_H * DN_DK, DN_H * DN_DK, DN_H * DN_DV, 2 * DN_H, 2 * DN_H, DN_H * DN_DV,
             2 * CONF_CH, N_BRANCH * D_MODEL)
P_IN = sum(IN_SPLITS)

kernel_name = 'hybrid_gqa_gdn_conformer_moe_dit'


def rms_norm(x, g):
    xf = x.astype(jnp.float32)
    y = xf * lax.rsqrt(jnp.mean(xf * xf, axis=-1, keepdims=True) + EPS)
    return (y * g.astype(jnp.float32)).astype(x.dtype)


def layer_norm(x, g, b):
    xf = x.astype(jnp.float32)
    mu = jnp.mean(xf, axis=-1, keepdims=True)
    var = jnp.mean(jnp.square(xf - mu), axis=-1, keepdims=True)
    y = (xf - mu) * lax.rsqrt(var + EPS) * g.astype(jnp.float32) + b.astype(jnp.float32)
    return y.astype(x.dtype)


def l2_norm(x):
    xf = x.astype(jnp.float32)
    return xf * lax.rsqrt(jnp.sum(xf * xf, axis=-1, keepdims=True) + EPS)


def modulate(h, shift, scale):
    return h * (1.0 + scale) + shift


def split_in(z):
    idx = []
    acc = 0
    for w in IN_SPLITS[:-1]:
        acc += w
        idx.append(acc)
    return jnp.split(z, idx, axis=-1)


def depthwise_conv(x, w):
    k = w.shape[0]
    return lax.conv_general_dilated(
        x, w[:, None, :], window_strides=(1,), padding=[((k - 1) // 2, k // 2)],
        dimension_numbers=('NWC', 'WIO', 'NWC'), feature_group_count=x.shape[-1])


def axial_rope_tables(rows):
    r, col = jnp.meshgrid(jnp.arange(rows, dtype=jnp.float32),
                          jnp.arange(GRID_W, dtype=jnp.float32), indexing='ij')
    inv = ROPE_THETA ** (-jnp.arange(ROPE_AXIS_PAIRS, dtype=jnp.float32) / ROPE_AXIS_PAIRS)
    ang = jnp.stack([r.reshape(-1)[:, None] * inv, col.reshape(-1)[:, None] * inv], axis=1)
    return jnp.cos(ang), jnp.sin(ang)


def apply_rope(x, cos, sin):
    shp = x.shape
    xr = x.reshape(shp[:3] + (2, 2, ROPE_AXIS_PAIRS)).astype(jnp.float32)
    x1, x2 = xr[..., 0, :], xr[..., 1, :]
    cs, sn = cos[None, :, None], sin[None, :, None]
    out = jnp.stack([x1 * cs - x2 * sn, x1 * sn + x2 * cs], axis=-2)
    return out.reshape(shp).astype(x.dtype)


def att_qkv(parts, qg, kg):
    lead = parts[0].shape[:-1]
    q = rms_norm(parts[0].reshape(lead + (ATT_HQ, ATT_DH)), qg)
    k = rms_norm(parts[1].reshape(lead + (ATT_HKV, ATT_DH)), kg)
    v = parts[2].reshape(lead + (ATT_HKV, ATT_DH))
    return q, k, v


def gqa_block(q, k, v):
    s = jnp.einsum('bqkgd,btkd->bkgqt', q, k).astype(jnp.float32) * (ATT_DH ** -0.5)
    p = jax.nn.softmax(s, axis=-1).astype(v.dtype)
    return jnp.einsum('bkgqt,btkd->bqkgd', p, v)


def attend_latent(q, k, v):
    b, s = q.shape[:2]
    nb = s // Q_BLOCK
    qb = q.reshape(b, nb, Q_BLOCK, ATT_HKV, ATT_GROUP, ATT_DH).swapaxes(0, 1)
    o = lax.map(lambda qq: gqa_block(qq, k, v), qb)
    return o.swapaxes(0, 1).reshape(b, s, ATT_HQ * ATT_DH)


def gated_delta_chunked(q, k, v, g, beta, state0):
    b, t, h, dk = q.shape
    n = t // DN_CHUNK
    f32 = jnp.float32

    def chunks(a):
        a = a.astype(f32).reshape((b, n, DN_CHUNK, h) + a.shape[3:])
        return jnp.moveaxis(a, (1, 3), (0, 2))

    qc = chunks(q) * (dk ** -0.5)
    kc, vc, gc, bc = chunks(k), chunks(v), chunks(g), chunks(beta)
    gcum = jnp.cumsum(gc, axis=-1)
    lower = jnp.tril(jnp.ones((DN_CHUNK, DN_CHUNK), dtype=bool))
    strict = jnp.tril(jnp.ones((DN_CHUNK, DN_CHUNK), dtype=bool), -1)
    decay = jnp.exp(jnp.where(lower, gcum[..., :, None] - gcum[..., None, :], -jnp.inf))
    kb = kc * bc[..., None]
    vb = vc * bc[..., None]
    lmat = jnp.where(strict, jnp.einsum('nbhid,nbhjd->nbhij', kb, kc) * decay, 0.0)
    eye = jnp.eye(DN_CHUNK, dtype=f32)
    tmat = lax.linalg.triangular_solve(lmat + eye, jnp.broadcast_to(eye, lmat.shape),
                                       left_side=True, lower=True, unit_diagonal=True)
    u = tmat @ vb
    w = tmat @ (kb * jnp.exp(gcum)[..., None])
    intra = jnp.where(lower, jnp.einsum('nbhid,nbhjd->nbhij', qc, kc) * decay, 0.0)

    def step(s_prev, xs):
        q_i, k_i, u_i, w_i, g_i, a_i = xs
        v_new = u_i - w_i @ s_prev
        o = (q_i * jnp.exp(g_i)[..., None]) @ s_prev + a_i @ v_new
        g_last = g_i[..., -1]
        k_dec = k_i * jnp.exp(g_last[..., None] - g_i)[..., None]
        s_new = s_prev * jnp.exp(g_last)[..., None, None] + jnp.einsum('bhcd,bhce->bhde', k_dec, v_new)
        return s_new, o

    s_fin, o = lax.scan(step, state0.astype(f32), (qc, kc, u, w, gcum, intra))
    o = jnp.moveaxis(o, (0, 2), (1, 3)).reshape(b, t, h, v.shape[-1])
    return o, s_fin


def dn_stream(parts, conv_w, a_log, dt_bias):
    qkv = jax.nn.silu(depthwise_conv(jnp.concatenate([parts[3], parts[4], parts[5]], axis=-1), conv_w))
    q, k, v = jnp.split(qkv, [DN_H * DN_DK, 2 * DN_H * DN_DK], axis=-1)
    lead = q.shape[:-1]
    q = l2_norm(q.reshape(lead + (DN_H, DN_DK)))
    k = l2_norm(k.reshape(lead + (DN_H, DN_DK)))
    v = v.reshape(lead + (DN_H, DN_DV))
    gshape = lead + (2, DN_H)
    beta = jax.nn.sigmoid(parts[6].reshape(gshape).astype(jnp.float32))
    g = -jnp.exp(a_log.astype(jnp.float32)) * jax.nn.softplus(
        parts[7].reshape(gshape).astype(jnp.float32) + dt_bias.astype(jnp.float32))
    return q, k, v, g, beta


def dn_bidir(q, k, v, g, beta, s_fwd, s_bwd):
    o_f, s_f = gated_delta_chunked(q, k, v, g[:, :, 0], beta[:, :, 0], s_fwd)
    rev = lambda a: jnp.flip(a, axis=1)
    o_b, s_b = gated_delta_chunked(rev(q), rev(k), rev(v), rev(g[:, :, 1]), rev(beta[:, :, 1]), s_bwd)
    return o_f + rev(o_b), s_f, s_b


def dn_out(o, z_gate, norm_g):
    o = o * lax.rsqrt(jnp.mean(o * o, axis=-1, keepdims=True) + EPS) * norm_g.astype(jnp.float32)
    o = o.reshape(z_gate.shape) * jax.nn.silu(z_gate.astype(jnp.float32))
    return o.astype(z_gate.dtype)


def conformer(z, dw_w, dw_b, ln_g, ln_b, w_pw):
    a, b = jnp.split(z, 2, axis=-1)
    u = depthwise_conv(a * jax.nn.sigmoid(b), dw_w) + dw_b
    u = jax.nn.silu(layer_norm(u, ln_g, ln_b))
    return u @ w_pw


def merge(y_att, y_dn, y_conf, z_gates, w_o):
    g = jax.nn.sigmoid(z_gates.reshape(z_gates.shape[:-1] + (N_BRANCH, D_MODEL)))
    y = g[..., 0, :] * y_att + g[..., 1, :] * y_dn + g[..., 2, :] * y_conf
    return y @ w_o


def moe(h, w_r, b_r, w_gu, b_gu, w_dn, b_dn):
    logits = (h @ w_r + b_r).astype(jnp.float32)
    top_v, top_i = lax.top_k(logits, TOP_K)
    top_w = jax.nn.softmax(top_v, axis=-1)
    combine = jnp.sum(jax.nn.one_hot(top_i, N_EXPERTS, dtype=jnp.float32) * top_w[..., None], axis=-2)
    out = jnp.zeros(h.shape, jnp.float32)
    for e in range(N_EXPERTS):
        gate, up = jnp.split(h @ w_gu[e] + b_gu[e], 2, axis=-1)
        gate = jnp.minimum(gate, SWIGLU_LIMIT)
        up = jnp.clip(up, -SWIGLU_LIMIT, SWIGLU_LIMIT)
        act = (up + 1.0) * (gate * jax.nn.sigmoid(SWIGLU_ALPHA * gate))
        out = out + combine[:, e:e + 1] * (act @ w_dn[e] + b_dn[e]).astype(jnp.float32)
    return out.astype(h.dtype)


def setup_inputs(seed: int = 0) -> dict:
    key = jax.random.key(seed)
    ks = jax.random.split(key, 32)
    L, D, f32 = DEPTH, D_MODEL, jnp.float32

    def nrm(i, shape, scale):
        return scale * jax.random.normal(ks[i], shape, f32)

    dt = jnp.exp(jax.random.uniform(ks[10], (L, 2, DN_H), f32, minval=math.log(1e-3), maxval=math.log(1e-1)))
    return {
        'x': nrm(0, (BATCH, SEQ, D), 1.0),
        'c': nrm(1, (BATCH, D), 1.0),
        'ctx': nrm(2, (BATCH, CTX_LEN, D), 1.0),
        'c_ctx': nrm(3, (D,), 1.0),
        'w_mod': nrm(4, (L, D, N_MOD * D), 0.5 * D ** -0.5),
        'b_mod': nrm(5, (L, N_MOD * D), 0.02),
        'norm1_g': 1.0 + nrm(6, (L, D), 0.02),
        'w_in': nrm(7, (L, D, P_IN), D ** -0.5),
        'q_norm_g': 1.0 + nrm(8, (L, ATT_DH), 0.02),
        'k_norm_g': 1.0 + nrm(9, (L, ATT_DH), 0.02),
        'dn_conv_w': nrm(11, (L, DN_CONV, 2 * DN_H * DN_DK + DN_H * DN_DV), DN_CONV ** -0.5),
        'dn_a_log': jnp.log(jax.random.uniform(ks[12], (L, 2, DN_H), f32, minval=1.0, maxval=16.0)),
        'dn_dt_bias': dt + jnp.log(-jnp.expm1(-dt)),
        'dn_norm_g': 1.0 + nrm(13, (L, DN_DV), 0.02),
        'conf_dw_w': nrm(14, (L, CONF_KW, CONF_CH), CONF_KW ** -0.5),
        'conf_dw_b': nrm(15, (L, CONF_CH), 0.02),
        'conf_ln_g': 1.0 + nrm(16, (L, CONF_CH), 0.02),
        'conf_ln_b': nrm(17, (L, CONF_CH), 0.02),
        'w_att_o': nrm(18, (L, ATT_HQ * ATT_DH, D), (ATT_HQ * ATT_DH) ** -0.5),
        'w_dn_o': nrm(19, (L, DN_H * DN_DV, D), (DN_H * DN_DV) ** -0.5),
        'w_conf_o': nrm(20, (L, CONF_CH, D), CONF_CH ** -0.5),
        'w_out': nrm(21, (L, D, D), D ** -0.5),
        'norm2_g': 1.0 + nrm(22, (L, D), 0.02),
        'w_router': nrm(23, (L, D, N_EXPERTS), D ** -0.5),
        'b_router': nrm(24, (L, N_EXPERTS), 0.01),
        'w_gate_up': nrm(25, (L, N_EXPERTS, D, 2 * D_FF), D ** -0.5),
        'b_gate_up': nrm(26, (L, N_EXPERTS, 2 * D_FF), 0.02),
        'w_down': nrm(27, (L, N_EXPERTS, D_FF, D), D_FF ** -0.5),
        'b_down': nrm(28, (L, N_EXPERTS, D), 0.02),
        'final_g': 1.0 + nrm(29, (D,), 0.02),
    }


def reference(x, c, ctx, c_ctx, w_mod, b_mod, norm1_g, w_in, q_norm_g, k_norm_g,
              dn_conv_w, dn_a_log, dn_dt_bias, dn_norm_g, conf_dw_w, conf_dw_b, conf_ln_g, conf_ln_b,
              w_att_o, w_dn_o, w_conf_o, w_out, norm2_g, w_router, b_router,
              w_gate_up, b_gate_up, w_down, b_down, final_g):
    b, s, d = x.shape
    n_ctx = ctx.shape[1]
    rows = s // GRID_W
    cos, sin = axial_rope_tables(rows)
    xc = ctx
    silu_c = jax.nn.silu(c)[:, None, :]
    silu_cc = jax.nn.silu(c_ctx)
    s0 = jnp.zeros((b, DN_H, DN_DK, DN_DV), jnp.float32)
    for l in range(DEPTH):
        ctx_needed = l < DEPTH - 1
        mx = jnp.split(silu_c @ w_mod[l] + b_mod[l], N_MOD, axis=-1)
        mc = jnp.split(silu_cc @ w_mod[l] + b_mod[l], N_MOD, axis=-1)

        hx = modulate(rms_norm(x, norm1_g[l]), mx[0], mx[1])
        hc = modulate(rms_norm(xc, norm1_g[l]), mc[0], mc[1])
        px = split_in(hx @ w_in[l])
        pc = split_in(hc @ w_in[l])

        qx, kx, vx = att_qkv(px, q_norm_g[l], k_norm_g[l])
        qx = apply_rope(qx, cos, sin)
        kx = apply_rope(kx, cos, sin)
        qc, kc, vc = att_qkv(pc, q_norm_g[l], k_norm_g[l])
        att_x = attend_latent(qx, jnp.concatenate([kc, kx], axis=1), jnp.concatenate([vc, vx], axis=1))

        qdc, kdc, vdc, gdc, bdc = dn_stream(pc, dn_conv_w[l], dn_a_log[l], dn_dt_bias[l])
        qdx, kdx, vdx, gdx, bdx = dn_stream(px, dn_conv_w[l], dn_a_log[l], dn_dt_bias[l])
        o_c, s_f, s_b = dn_bidir(qdc, kdc, vdc, gdc, bdc, s0, s0)
        o_x, _, _ = dn_bidir(qdx, kdx, vdx, gdx, bdx, s_f, s_b)

        y_mix_x = merge(att_x @ w_att_o[l],
                        dn_out(o_x, px[8], dn_norm_g[l]) @ w_dn_o[l],
                        conformer(px[9], conf_dw_w[l], conf_dw_b[l], conf_ln_g[l], conf_ln_b[l], w_conf_o[l]),
                        px[10], w_out[l])
        x = x + mx[2] * y_mix_x
        if ctx_needed:
            att_c = gqa_block(qc.reshape(b, n_ctx, ATT_HKV, ATT_GROUP, ATT_DH), kc, vc).reshape(b, n_ctx, ATT_HQ * ATT_DH)
            y_mix_c = merge(att_c @ w_att_o[l],
                            dn_out(o_c, pc[8], dn_norm_g[l]) @ w_dn_o[l],
                            conformer(pc[9], conf_dw_w[l], conf_dw_b[l], conf_ln_g[l], conf_ln_b[l], w_conf_o[l]),
                            pc[10], w_out[l])
            xc = xc + mc[2] * y_mix_c

        hx2 = modulate(rms_norm(x, norm2_g[l]), mx[3], mx[4]).reshape(b * s, d)
        if ctx_needed:
            hc2 = modulate(rms_norm(xc, norm2_g[l]), mc[3], mc[4]).reshape(b * n_ctx, d)
            y = moe(jnp.concatenate([hx2, hc2], axis=0), w_router[l], b_router[l],
                    w_gate_up[l], b_gate_up[l], w_down[l], b_down[l])
            x = x + mx[5] * y[:b * s].reshape(b, s, d)
            xc = xc + mc[5] * y[b * s:].reshape(b, n_ctx, d)
        else:
            y = moe(hx2, w_router[l], b_router[l], w_gate_up[l], b_gate_up[l], w_down[l], b_down[l])
            x = x + mx[5] * y.reshape(b, s, d)
    return rms_norm(x, final_g)
```

```python
import functools

import jax
import jax.numpy as jnp
from jax import lax
from jax.experimental import pallas as pl
from jax.experimental.pallas import tpu as pltpu

F32 = jnp.float32
BF16 = jnp.bfloat16
I32 = jnp.int32
U32 = jnp.uint32

D_MODEL = 1024
N_MOD = 6
EPS = 1e-6
GRID_W = 64
ATT_HQ, ATT_HKV, ATT_DH = 8, 2, 64
ATT_GROUP = ATT_HQ // ATT_HKV
ROPE_THETA = 10000.0
ROPE_PAIRS = ATT_DH // 4
DN_H, DN_DK, DN_DV, DN_CONV, DN_CHUNK = 4, 128, 128, 4, 64
CONF_CH, CONF_KW = 512, 31
N_EXPERTS, TOP_K, D_FF = 32, 4, 1024
SWIGLU_LIMIT, SWIGLU_ALPHA = 7.0, 1.702
N_BRANCH = 3

LANES = 128
SUBLANES = 8
TR = 256
TM = 256
CONF_TR = 128
CONF_HALO = 16
ATT_W = ATT_HQ * ATT_DH + 2 * ATT_HKV * ATT_DH
ATT_QK = ATT_HQ * ATT_DH + ATT_HKV * ATT_DH
DN_W = 3 * DN_H * DN_DK
SM_W = DN_H * LANES
VMEM_LIMIT = 52 * 1024 * 1024


def _cparams(sem, vmem=None):
    return pltpu.CompilerParams(dimension_semantics=sem, vmem_limit_bytes=vmem)


def _dot(a, b):
    return jnp.dot(a.astype(BF16), b.astype(BF16), preferred_element_type=F32)


def _dot_nt(a, b):
    return lax.dot_general(a.astype(BF16), b.astype(BF16), (((1,), (1,)), ((), ())),
                           preferred_element_type=F32)


def _dot_tn(a, b):
    return lax.dot_general(a.astype(BF16), b.astype(BF16), (((0,), (0,)), ((), ())),
                           preferred_element_type=F32)


def _split2(x):
    hi = x.astype(BF16)
    lo = (x - hi.astype(F32)).astype(BF16)
    return hi, lo


def _split3(x):
    hi = x.astype(BF16)
    r = x - hi.astype(F32)
    mid = r.astype(BF16)
    lo = (r - mid.astype(F32)).astype(BF16)
    return hi, mid, lo


def _dot_sel_l(sel, x):
    h, m, l = _split3(x)
    return (jnp.dot(sel, h, preferred_element_type=F32) + jnp.dot(sel, m, preferred_element_type=F32)
            + jnp.dot(sel, l, preferred_element_type=F32))


def _dot_sel_r(x, sel):
    h, m, l = _split3(x)
    return (jnp.dot(h, sel, preferred_element_type=F32) + jnp.dot(m, sel, preferred_element_type=F32)
            + jnp.dot(l, sel, preferred_element_type=F32))


def _dot_hl(a, b):
    ah, al = _split2(a)
    bh, bl = _split2(b)
    return (jnp.dot(ah, bh, preferred_element_type=F32) + jnp.dot(ah, bl, preferred_element_type=F32)
            + jnp.dot(al, bh, preferred_element_type=F32))


def _sigmoid(x):
    return jax.nn.sigmoid(x)


def _silu(x):
    return x * jax.nn.sigmoid(x)


def _softplus(x):
    return jnp.maximum(x, 0.0) + jnp.log1p(jnp.exp(-jnp.abs(x)))


def _rms_mod(x, g, shift, scale):
    ms = jnp.mean(x * x, axis=-1, keepdims=True)
    y = x * lax.rsqrt(ms + EPS) * g
    return y * (1.0 + scale) + shift


def _iota(shape, dim):
    return lax.broadcasted_iota(I32, shape, dim)


def _mod_kernel(cc_ref, w_ref, b_ref, o_ref):
    o_ref[...] = _dot_hl(_silu(cc_ref[...]), w_ref[...]) + b_ref[...]


def _mod_all(cc, w_mod, b_mod):
    n_l, d, n = w_mod.shape
    tn = 1536
    rows = cc.shape[0]
    return pl.pallas_call(
        _mod_kernel,
        grid=(n_l, n // tn),
        in_specs=[pl.BlockSpec((rows, d), lambda l, j: (0, 0)),
                  pl.BlockSpec((None, d, tn), lambda l, j: (l, 0, j)),
                  pl.BlockSpec((None, 1, tn), lambda l, j: (l, 0, j))],
        out_specs=pl.BlockSpec((None, rows, tn), lambda l, j: (l, 0, j)),
        out_shape=jax.ShapeDtypeStruct((n_l, rows, n), F32),
        compiler_params=_cparams(("parallel", "parallel"), VMEM_LIMIT),
        name="mod_all",
    )(cc, w_mod, b_mod.reshape(n_l, 1, n))


def _inproj_kernel(x_ref, mod_ref, g1_ref, watt_ref, wdn_ref, wsm_ref, wgate_ref, wconf_ref,
                   qkg_ref, cos_ref, sin_ref, bavg_ref, alog_ref, dtb_ref,
                   att_ref, dnqkv_ref, sm_ref, gate_ref, glu_ref):
    h = _rms_mod(x_ref[...], g1_ref[...], mod_ref[0:1, :], mod_ref[1:2, :]).astype(BF16)

    za = jnp.dot(h, watt_ref[...], preferred_element_type=F32)
    cs = cos_ref[...]
    sn = sin_ref[...]
    bavg = bavg_ref[...]
    first = (_iota((TR, LANES), 1) & 31) < ROPE_PAIRS
    n_q_blocks = ATT_HQ * ATT_DH // LANES
    for j in range(ATT_QK // LANES):
        sl = slice(LANES * j, LANES * (j + 1))
        blk = za[:, sl]
        hi, lo = _split2(blk * blk)
        ms = jnp.dot(hi, bavg, preferred_element_type=F32) + jnp.dot(lo, bavg, preferred_element_type=F32)
        y = blk * lax.rsqrt(ms + EPS) * qkg_ref[:, sl]
        rot = jnp.where(first, pltpu.roll(y, LANES - ROPE_PAIRS, 1), pltpu.roll(y, ROPE_PAIRS, 1))
        y = y * cs + rot * sn
        if j < n_q_blocks:
            y = y * (ATT_DH ** -0.5)
        att_ref[:, sl] = y.astype(BF16)
    att_ref[:, ATT_QK:ATT_W] = za[:, ATT_QK:ATT_W].astype(BF16)

    dnqkv_ref[...] = jnp.dot(h, wdn_ref[...], preferred_element_type=F32)

    zs = jnp.dot(h, wsm_ref[...], preferred_element_type=F32)
    ln = _iota(zs.shape, 1) & (LANES - 1)
    g = -jnp.exp(alog_ref[...]) * _softplus(zs + dtb_ref[...])
    sm_ref[...] = jnp.where(ln < 2, _sigmoid(zs), jnp.where(ln < 4, g, 0.0))

    gate_ref[...] = jnp.dot(h, wgate_ref[...], preferred_element_type=F32)

    zc = jnp.dot(h, wconf_ref[...], preferred_element_type=F32)
    glu_ref[...] = zc[:, :CONF_CH] * _sigmoid(zc[:, CONF_CH:])


def _inproj(xa, mod_l, g1, watt, wdn, wsm, wgate, wconf, qkg, cos_t, sin_t, bavg, alog, dtb):
    b, t, d = xa.shape
    nt = t // TR
    row = lambda bi, ti: (bi, ti, 0)
    const = lambda bi, ti: (0, 0)
    full = lambda a: pl.BlockSpec(a.shape, const)
    return pl.pallas_call(
        _inproj_kernel,
        grid=(b, nt),
        in_specs=[pl.BlockSpec((None, TR, d), row),
                  pl.BlockSpec((None, N_MOD, d), lambda bi, ti: (jnp.where(ti == 0, b, bi), 0, 0)),
                  full(g1), full(watt), full(wdn), full(wsm), full(wgate), full(wconf), full(qkg),
                  pl.BlockSpec((TR, LANES), lambda bi, ti: (ti, 0)),
                  pl.BlockSpec((TR, LANES), lambda bi, ti: (ti, 0)),
                  full(bavg), full(alog), full(dtb)],
        out_specs=[pl.BlockSpec((None, TR, ATT_W), row),
                   pl.BlockSpec((None, TR, DN_W), row),
                   pl.BlockSpec((None, TR, SM_W), row),
                   pl.BlockSpec((None, TR, DN_H * DN_DV), row),
                   pl.BlockSpec((None, TR, CONF_CH), row)],
        out_shape=[jax.ShapeDtypeStruct((b, t, ATT_W), BF16),
                   jax.ShapeDtypeStruct((b, t, DN_W), F32),
                   jax.ShapeDtypeStruct((b, t, SM_W), F32),
                   jax.ShapeDtypeStruct((b, t, DN_H * DN_DV), F32),
                   jax.ShapeDtypeStruct((b, t, CONF_CH), F32)],
        compiler_params=_cparams(("parallel", "parallel"), VMEM_LIMIT),
        name="inproj",
    )(xa, mod_l, g1, watt, wdn, wsm, wgate, wconf, qkg, cos_t, sin_t, bavg, alog, dtb)


def _attn_kernel(q_ref, kv_ref, o_ref, *, n_keys):
    def run(nk):
        for kvh in range(ATT_HKV):
            k = kv_ref[0:nk, ATT_HQ * ATT_DH + ATT_DH * kvh:ATT_HQ * ATT_DH + ATT_DH * (kvh + 1)]
            v = kv_ref[0:nk, ATT_QK + ATT_DH * kvh:ATT_QK + ATT_DH * (kvh + 1)]
            for gi in range(ATT_GROUP):
                hq = kvh * ATT_GROUP + gi
                q = q_ref[:, ATT_DH * hq:ATT_DH * (hq + 1)]
                s = lax.dot_general(q, k, (((1,), (1,)), ((), ())), preferred_element_type=F32)
                p = jnp.exp(s - jnp.max(s, axis=-1, keepdims=True))
                l = jnp.sum(p, axis=-1, keepdims=True)
                o = jnp.dot(p.astype(BF16), v, preferred_element_type=F32) / l
                o_ref[:, ATT_DH * hq:ATT_DH * (hq + 1)] = o.astype(BF16)

    @pl.when(pl.program_id(1) == 0)
    def _():
        run(TR)

    @pl.when(pl.program_id(1) != 0)
    def _():
        run(n_keys)


def _attention(att):
    b, t, _ = att.shape
    nt = t // TR
    return pl.pallas_call(
        functools.partial(_attn_kernel, n_keys=t),
        grid=(b, nt),
        in_specs=[pl.BlockSpec((None, TR, ATT_W), lambda bi, ti: (bi, ti, 0)),
                  pl.BlockSpec((None, t, ATT_W), lambda bi, ti: (bi, 0, 0))],
        out_specs=pl.BlockSpec((None, TR, ATT_HQ * ATT_DH), lambda bi, ti: (bi, ti, 0)),
        out_shape=jax.ShapeDtypeStruct((b, t, ATT_HQ * ATT_DH), BF16),
        compiler_params=_cparams(("parallel", "parallel"), VMEM_LIMIT),
        name="attention",
    )(att, att)


def _dnprep_kernel(q_ref, k_ref, v_ref, wq_ref, wk_ref, wv_ref, sm_ref,
                   u_ref, w_ref, qg_ref, kd_ref, in_ref, egl_ref,
                   qs_ref, ks_ref, vs_ref, *, n_ctx):
    t = q_ref.shape[0]
    row = _iota((t, LANES), 0)
    seg_first = (row == 0) | (row == n_ctx)
    seg_last = (row == n_ctx - 1) | (row == t - 1)
    seg_last2 = seg_last | (row == n_ctx - 2) | (row == t - 2)

    def conv_silu(x_ref, w_ref_):
        x = x_ref[...]
        xm1 = jnp.where(seg_first, 0.0, pltpu.roll(x, 1, 0))
        xp1 = jnp.where(seg_last, 0.0, pltpu.roll(x, t - 1, 0))
        xp2 = jnp.where(seg_last2, 0.0, pltpu.roll(x, t - 2, 0))
        y = w_ref_[0:1, :] * xm1 + w_ref_[1:2, :] * x + w_ref_[2:3, :] * xp1 + w_ref_[3:4, :] * xp2
        return _silu(y)

    def l2n(x):
        return x * lax.rsqrt(jnp.sum(x * x, axis=-1, keepdims=True) + EPS)

    qs_ref[...] = l2n(conv_silu(q_ref, wq_ref)) * (DN_DK ** -0.5)
    ks_ref[...] = l2n(conv_silu(k_ref, wk_ref))
    vs_ref[...] = conv_silu(v_ref, wv_ref)

    c = DN_CHUNK
    ri = _iota((c, c), 0)
    ci = _iota((c, c), 1)
    lower_incl = ri >= ci
    upper_incl = ri <= ci
    eye = jnp.where(ri == ci, 1.0, 0.0).astype(F32)
    ones_b = jnp.ones((c, c), BF16)

    def chunk(cidx, carry):
        r0 = pl.multiple_of(cidx * c, c)
        q = qs_ref[pl.ds(r0, c), :]
        k = ks_ref[pl.ds(r0, c), :]
        v = vs_ref[pl.ds(r0, c), :]
        smc = sm_ref[pl.ds(r0, c), :]
        intra_parts = []
        for d in (0, 1):
            incl = lower_incl if d == 0 else upper_incl
            incl_t = upper_incl if d == 0 else lower_incl
            strict = (ri > ci) if d == 0 else (ri < ci)
            beta = smc[:, d:d + 1]
            g = smc[:, 2 + d:3 + d]
            gb = jnp.broadcast_to(g, (c, c))
            col = _dot_sel_l(jnp.where(incl, 1.0, 0.0).astype(BF16), gb)
            rowm = _dot_sel_l(ones_b, jnp.where(incl_t, gb, 0.0))
            decay = jnp.exp(jnp.where(incl, col - rowm, -jnp.inf))
            gc = col[:, 0:1]
            eg = jnp.exp(gc)
            gl = col[c - 1:c, 0:1] if d == 0 else col[0:1, 0:1]
            kb = k * beta
            vb = v * beta
            lmat = jnp.where(strict, _dot_nt(kb, k) * decay, 0.0)
            tm = eye - lmat
            pw = lmat
            for _ in range(5):
                pw = _dot_hl(pw, pw)
                tm = tm + _dot_hl(tm, pw)
            u_ref[d, pl.ds(r0, c), :] = _dot(tm, vb)
            w_ref[d, pl.ds(r0, c), :] = _dot(tm, kb * eg).astype(BF16)
            qg_ref[d, pl.ds(r0, c), :] = (q * eg).astype(BF16)
            kd_ref[d, pl.ds(r0, c), :] = (k * jnp.exp(gl - gc)).astype(BF16)
            intra_parts.append(jnp.where(incl, _dot_nt(q, k) * decay, 0.0))
            e0 = pl.multiple_of(cidx * SUBLANES, SUBLANES)
            egl_ref[d, pl.ds(e0, SUBLANES), :] = jnp.broadcast_to(jnp.exp(gl), (SUBLANES, LANES))
        in_ref[pl.ds(r0, c), :] = jnp.concatenate(intra_parts, axis=1).astype(BF16)
        return carry

    lax.fori_loop(0, t // c, chunk, 0)


def _dnprep(dnqkv, conv_w, sm, n_ctx):
    b, t, _ = dnqkv.shape
    nc = t // DN_CHUNK
    col = lambda off: pl.BlockSpec((None, t, LANES), lambda bi, hi: (bi, 0, off + hi))
    wcol = lambda off: pl.BlockSpec((DN_CONV, LANES), lambda bi, hi: (0, off + hi))
    big = pl.BlockSpec((None, None, 2, t, LANES), lambda bi, hi: (bi, hi, 0, 0, 0))
    return pl.pallas_call(
        functools.partial(_dnprep_kernel, n_ctx=n_ctx),
        grid=(b, DN_H),
        in_specs=[col(0), col(DN_H), col(2 * DN_H), wcol(0), wcol(DN_H), wcol(2 * DN_H),
                  pl.BlockSpec((None, t, LANES), lambda bi, hi: (bi, 0, hi))],
        out_specs=[big, big, big, big,
                   pl.BlockSpec((None, None, t, LANES), lambda bi, hi: (bi, hi, 0, 0)),
                   pl.BlockSpec((None, None, 2, nc * SUBLANES, LANES), lambda bi, hi: (bi, hi, 0, 0, 0))],
        out_shape=[jax.ShapeDtypeStruct((b, DN_H, 2, t, LANES), F32),
                   jax.ShapeDtypeStruct((b, DN_H, 2, t, LANES), BF16),
                   jax.ShapeDtypeStruct((b, DN_H, 2, t, LANES), BF16),
                   jax.ShapeDtypeStruct((b, DN_H, 2, t, LANES), BF16),
                   jax.ShapeDtypeStruct((b, DN_H, t, LANES), BF16),
                   jax.ShapeDtypeStruct((b, DN_H, 2, nc * SUBLANES, LANES), F32)],
        scratch_shapes=[pltpu.VMEM((t, LANES), F32)] * 3,
        compiler_params=_cparams(("parallel", "parallel"), VMEM_LIMIT),
        name="dn_prep",
    )(dnqkv, dnqkv, dnqkv, conv_w, conv_w, conv_w, sm)


def _dnscan_kernel(uf_ref, wf_ref, qgf_ref, kdf_ref, inf_ref, egf_ref,
                   ub_ref, wb_ref, qgb_ref, kdb_ref, inb_ref, egb_ref,
                   of_ref, ob_ref, s_ref):
    @pl.when(pl.program_id(1) == 0)
    def _():
        s_ref[...] = jnp.zeros_like(s_ref)

    c = DN_CHUNK
    n_cc = TR // c
    fwd = (uf_ref, wf_ref, qgf_ref, kdf_ref, inf_ref, egf_ref, of_ref)
    bwd = (ub_ref, wb_ref, qgb_ref, kdb_ref, inb_ref, egb_ref, ob_ref)

    def step(cc, carry):
        for h in range(DN_H):
            for d in (0, 1):
                u_r, w_r, qg_r, kd_r, in_r, eg_r, o_r = fwd if d == 0 else bwd
                ch = cc if d == 0 else n_cc - 1 - cc
                r0 = pl.multiple_of(ch * c, c)
                e0 = pl.multiple_of(ch * SUBLANES, SUBLANES)
                s = s_ref[2 * h + d]
                sb = s.astype(BF16)
                vn = (u_r[h, pl.ds(r0, c), :]
                      - jnp.dot(w_r[h, pl.ds(r0, c), :], sb, preferred_element_type=F32)).astype(BF16)
                intra = in_r[h, pl.ds(r0, c), :][:, d * c:(d + 1) * c]
                o = (jnp.dot(qg_r[h, pl.ds(r0, c), :], sb, preferred_element_type=F32)
                     + jnp.dot(intra, vn, preferred_element_type=F32))
                s_ref[2 * h + d] = (s * eg_r[h, pl.ds(e0, 1), :]
                                    + _dot_tn(kd_r[h, pl.ds(r0, c), :], vn))
                o_r[pl.ds(r0, c), LANES * h:LANES * (h + 1)] = o
        return carry

    lax.fori_loop(0, n_cc, step, 0)


def _dnscan(u, w, qg, kd, intra, egl):
    b, _, _, t, _ = u.shape
    nt = t // TR
    epr = (TR // DN_CHUNK) * SUBLANES
    tb = lambda ti: jnp.where(ti == 0, 0, nt - ti)
    big_f = pl.BlockSpec((None, DN_H, None, TR, LANES), lambda bi, ti: (bi, 0, 0, ti, 0))
    big_b = pl.BlockSpec((None, DN_H, None, TR, LANES), lambda bi, ti: (bi, 0, 1, tb(ti), 0))
    in_f = pl.BlockSpec((None, DN_H, TR, LANES), lambda bi, ti: (bi, 0, ti, 0))
    in_b = pl.BlockSpec((None, DN_H, TR, LANES), lambda bi, ti: (bi, 0, tb(ti), 0))
    eg_f = pl.BlockSpec((None, DN_H, None, epr, LANES), lambda bi, ti: (bi, 0, 0, ti, 0))
    eg_b = pl.BlockSpec((None, DN_H, None, epr, LANES), lambda bi, ti: (bi, 0, 1, tb(ti), 0))
    width = DN_H * DN_DV
    return pl.pallas_call(
        _dnscan_kernel,
        grid=(b, nt),
        in_specs=[big_f, big_f, big_f, big_f, in_f, eg_f, big_b, big_b, big_b, big_b, in_b, eg_b],
        out_specs=[pl.BlockSpec((None, TR, width), lambda bi, ti: (bi, ti, 0)),
                   pl.BlockSpec((None, TR, width), lambda bi, ti: (bi, tb(ti), 0))],
        out_shape=[jax.ShapeDtypeStruct((b, t, width), F32)] * 2,
        scratch_shapes=[pltpu.VMEM((2 * DN_H, DN_DK, DN_DV), F32)],
        compiler_params=_cparams(("parallel", "arbitrary"), VMEM_LIMIT),
        name="dn_scan",
    )(u, w, qg, kd, intra, egl, u, w, qg, kd, intra, egl)


def _conf_kernel(glu_ref, dww_ref, dwb_ref, lng_ref, lnb_ref, o_ref, pad_ref, u_ref, *, n_ctx):
    t = glu_ref.shape[0]
    halo = CONF_HALO
    lat0 = halo + n_ctx + 2 * halo
    zeros = lambda n: jnp.zeros((n, CONF_CH), F32)
    pad_ref[0:halo, :] = zeros(halo)
    pad_ref[halo + n_ctx:lat0, :] = zeros(2 * halo)
    pad_ref[lat0 + (t - n_ctx):lat0 + (t - n_ctx) + halo, :] = zeros(halo)
    pad_ref[halo:halo + n_ctx, :] = glu_ref[0:n_ctx, :]
    pad_ref[lat0:lat0 + (t - n_ctx), :] = glu_ref[n_ctx:t, :]

    n_ctx_tiles = n_ctx // CONF_TR
    blk_rows = CONF_TR + 2 * halo
    center = (CONF_KW - 1) // 2

    def conv_tile(i, carry):
        t0 = pl.multiple_of(i * CONF_TR, CONF_TR)
        p0 = pl.multiple_of(t0 + jnp.where(i >= n_ctx_tiles, 2 * halo, 0), SUBLANES)
        for lb in range(CONF_CH // LANES):
            sl = slice(LANES * lb, LANES * (lb + 1))
            blk = pad_ref[pl.ds(p0, blk_rows), sl]
            acc = jnp.zeros((CONF_TR, LANES), F32)
            for r in range(SUBLANES):
                rolled = blk if r == 0 else pltpu.roll(blk, blk_rows - r, 0)
                for j in range(CONF_KW):
                    off = halo + j - center
                    if off % SUBLANES != r:
                        continue
                    a8 = off - r
                    acc = acc + dww_ref[j:j + 1, sl] * rolled[a8:a8 + CONF_TR, :]
            u_ref[pl.ds(t0, CONF_TR), sl] = acc + dwb_ref[:, sl]
        return carry

    lax.fori_loop(0, t // CONF_TR, conv_tile, 0)

    def ln_tile(i, carry):
        t0 = pl.multiple_of(i * CONF_TR, CONF_TR)
        x = u_ref[pl.ds(t0, CONF_TR), :]
        mu = jnp.mean(x, axis=-1, keepdims=True)
        xc = x - mu
        var = jnp.mean(xc * xc, axis=-1, keepdims=True)
        y = xc * lax.rsqrt(var + EPS) * lng_ref[...] + lnb_ref[...]
        o_ref[pl.ds(t0, CONF_TR), :] = _silu(y).astype(BF16)
        return carry

    lax.fori_loop(0, t // CONF_TR, ln_tile, 0)


def _conformer(glu, dww, dwb, lng, lnb, n_ctx):
    b, t, ch = glu.shape
    pad_rows = t + 4 * CONF_HALO
    const = lambda bi: (0, 0)
    return pl.pallas_call(
        functools.partial(_conf_kernel, n_ctx=n_ctx),
        grid=(b,),
        in_specs=[pl.BlockSpec((None, t, ch), lambda bi: (bi, 0, 0)),
                  pl.BlockSpec(dww.shape, const), pl.BlockSpec(dwb.shape, const),
                  pl.BlockSpec(lng.shape, const), pl.BlockSpec(lnb.shape, const)],
        out_specs=pl.BlockSpec((None, t, ch), lambda bi: (bi, 0, 0)),
        out_shape=jax.ShapeDtypeStruct((b, t, ch), BF16),
        scratch_shapes=[pltpu.VMEM((pad_rows, ch), F32), pltpu.VMEM((t, ch), F32)],
        compiler_params=_cparams(("parallel",), VMEM_LIMIT),
        name="conformer",
    )(glu, dww, dwb, lng, lnb)


def _merge_kernel(x_ref, mod_ref, g1_ref, g2_ref, att_ref, of_ref, ob_ref, gate_ref, cu_ref,
                  wg_ref, wao_ref, wdo_ref, wco_ref, wout_ref, dng_ref, wr_ref, br_ref,
                  xo_ref, hp_ref, idx_ref, wts_ref):
    x = x_ref[...]
    h = _rms_mod(x, g1_ref[...], mod_ref[0:1, :], mod_ref[1:2, :]).astype(BF16)

    o = of_ref[...] + ob_ref[...]
    gate = gate_ref[...]
    dn_parts = []
    for hd in range(DN_H):
        sl = slice(LANES * hd, LANES * (hd + 1))
        oh = o[:, sl]
        oh = oh * lax.rsqrt(jnp.mean(oh * oh, axis=-1, keepdims=True) + EPS) * dng_ref[...]
        dn_parts.append((oh * _silu(gate[:, sl])).astype(BF16))
    dn_o = jnp.concatenate(dn_parts, axis=1)

    y_att = jnp.dot(att_ref[...], wao_ref[...], preferred_element_type=F32)
    y_dn = jnp.dot(dn_o, wdo_ref[...], preferred_element_type=F32)
    y_cf = jnp.dot(cu_ref[...], wco_ref[...], preferred_element_type=F32)
    d = D_MODEL
    y = (_sigmoid(jnp.dot(h, wg_ref[:, 0:d], preferred_element_type=F32)) * y_att
         + _sigmoid(jnp.dot(h, wg_ref[:, d:2 * d], preferred_element_type=F32)) * y_dn
         + _sigmoid(jnp.dot(h, wg_ref[:, 2 * d:3 * d], preferred_element_type=F32)) * y_cf)
    x1 = x + mod_ref[2:3, :] * jnp.dot(y.astype(BF16), wout_ref[...], preferred_element_type=F32)
    xo_ref[...] = x1

    h2 = _rms_mod(x1, g2_ref[...], mod_ref[3:4, :], mod_ref[4:5, :])
    hb = h2.astype(BF16).astype(F32)
    w_hi = pltpu.bitcast(hb[:, :d // 2], U32) & jnp.uint32(0xFFFF0000)
    w_lo = pltpu.bitcast(hb[:, d // 2:], U32) >> 16
    hp_ref[...] = w_hi | w_lo

    lane = _iota((TR, LANES), 1)
    lanef = lane.astype(F32)
    logits = jnp.where(lane < N_EXPERTS, _dot_hl(h2, wr_ref[...]) + br_ref[...], -jnp.inf)
    vals, ids = [], []
    for _ in range(TOP_K):
        m = jnp.max(logits, axis=-1, keepdims=True)
        am = jnp.min(jnp.where(logits == m, lanef, float(LANES)), axis=-1, keepdims=True)
        vals.append(m)
        ids.append(am)
        logits = jnp.where(lanef == am, -jnp.inf, logits)
    es = [jnp.exp(v - vals[0]) for v in vals]
    den = es[0] + es[1] + es[2] + es[3]
    idx_o = jnp.zeros((TR, LANES), F32)
    wts_o = jnp.zeros((TR, LANES), F32)
    for k in range(TOP_K):
        idx_o = jnp.where(lane == k, ids[k], idx_o)
        wts_o = jnp.where(lane == k, es[k] / den, wts_o)
    idx_ref[...] = idx_o.astype(I32)
    wts_ref[...] = wts_o


def _merge(xa, mod_l, g1, g2, att_o, o_f, o_b, gate, conf_u, wg, wao, wdo, wco, wout, dng, wr, br):
    b, t, d = xa.shape
    nt = t // TR
    row = lambda bi, ti: (bi, ti, 0)
    const = lambda bi, ti: (0, 0)
    full = lambda a: pl.BlockSpec(a.shape, const)
    rowspec = lambda w: pl.BlockSpec((None, TR, w), row)
    return pl.pallas_call(
        _merge_kernel,
        grid=(b, nt),
        in_specs=[rowspec(d),
                  pl.BlockSpec((None, N_MOD, d), lambda bi, ti: (jnp.where(ti == 0, b, bi), 0, 0)),
                  full(g1), full(g2), rowspec(ATT_HQ * ATT_DH), rowspec(DN_H * DN_DV), rowspec(DN_H * DN_DV),
                  rowspec(DN_H * DN_DV), rowspec(CONF_CH),
                  full(wg), full(wao), full(wdo), full(wco), full(wout), full(dng), full(wr), full(br)],
        out_specs=[rowspec(d), rowspec(d // 2), rowspec(LANES), rowspec(LANES)],
        out_shape=[jax.ShapeDtypeStruct((b, t, d), F32),
                   jax.ShapeDtypeStruct((b, t, d // 2), U32),
                   jax.ShapeDtypeStruct((b, t, LANES), I32),
                   jax.ShapeDtypeStruct((b, t, LANES), F32)],
        compiler_params=_cparams(("parallel", "parallel"), VMEM_LIMIT),
        name="merge_router",
    )(xa, mod_l, g1, g2, att_o, o_f, o_b, gate, conf_u, wg, wao, wdo, wco, wout, dng, wr, br)


def _rank_kernel(idx_ref, rank_ref, cnt_ref, carry_ref):
    @pl.when(pl.program_id(0) == 0)
    def _():
        carry_ref[...] = jnp.zeros_like(carry_ref)

    idx = idx_ref[...]
    lane = _iota((TR, LANES), 1)
    ohs = [idx[:, k:k + 1] == lane for k in range(TOP_K)]
    cnt = jnp.zeros((TR, LANES), F32)
    for oh in ohs:
        cnt = cnt + jnp.where(oh, 1.0, 0.0)
    strict = jnp.where(_iota((TR, TR), 1) < _iota((TR, TR), 0), 1.0, 0.0).astype(BF16)
    before = jnp.dot(strict, cnt.astype(BF16), preferred_element_type=F32) + carry_ref[0:1, :]
    rank = jnp.zeros((TR, LANES), F32)
    for k, oh in enumerate(ohs):
        rk = jnp.sum(jnp.where(oh, before, 0.0), axis=-1, keepdims=True)
        rank = jnp.where(lane == k, rk, rank)
    rank_ref[...] = rank
    tot = carry_ref[0:1, :] + jnp.sum(cnt, axis=0, keepdims=True)
    carry_ref[...] = jnp.broadcast_to(tot, carry_ref.shape)
    cnt_ref[...] = jnp.broadcast_to(tot, cnt_ref.shape)


def _rank(idx2):
    n = idx2.shape[0]
    return pl.pallas_call(
        _rank_kernel,
        grid=(n // TR,),
        in_specs=[pl.BlockSpec((TR, LANES), lambda i: (i, 0))],
        out_specs=[pl.BlockSpec((TR, LANES), lambda i: (i, 0)),
                   pl.BlockSpec((SUBLANES, LANES), lambda i: (0, 0))],
        out_shape=[jax.ShapeDtypeStruct((n, LANES), F32),
                   jax.ShapeDtypeStruct((SUBLANES, LANES), F32)],
        scratch_shapes=[pltpu.VMEM((SUBLANES, LANES), F32)],
        compiler_params=_cparams(("arbitrary",)),
        name="moe_rank",
    )(idx2)


def _pos_kernel(idx_ref, rank_ref, cnt_ref, pos_ref, meta_ref, *, n_tiles_pad):
    cnt = cnt_ref[...]
    padded = jnp.floor((cnt + float(TM - 1)) * (1.0 / TM)) * float(TM)
    incl = jnp.where(_iota((LANES, LANES), 0) <= _iota((LANES, LANES), 1), 1.0, 0.0).astype(BF16)
    end = _dot_sel_r(padded, incl)
    start = (end - padded)[0:1, :]

    idx = idx_ref[...]
    rank = rank_ref[...]
    lane = _iota((TR, LANES), 1)
    pos = jnp.zeros((TR, LANES), F32)
    for k in range(TOP_K):
        st = jnp.sum(jnp.where(idx[:, k:k + 1] == lane, start, 0.0), axis=-1, keepdims=True)
        pos = jnp.where(lane == k, st + rank[:, k:k + 1], pos)
    pos_ref[...] = pos.astype(I32)

    end_t = jnp.transpose(jnp.broadcast_to(end[0:1, :], (LANES, LANES)))
    end_w = jnp.concatenate([end_t] * (n_tiles_pad // LANES), axis=1)
    tile0 = _iota((LANES, n_tiles_pad), 1).astype(F32) * float(TM)
    valid_e = _iota((LANES, n_tiles_pad), 0) < N_EXPERTS
    te = jnp.sum(jnp.where(valid_e & (end_w <= tile0), 1.0, 0.0), axis=0, keepdims=True)
    te = jnp.minimum(te, float(N_EXPERTS - 1))
    n_used = end[0:1, N_EXPERTS - 1:N_EXPERTS] * (1.0 / TM)
    gend = jnp.concatenate([end[0:1, :]] + [jnp.zeros((1, n_tiles_pad - LANES), F32)], axis=1)
    r8 = _iota((SUBLANES, n_tiles_pad), 0)
    meta = jnp.where(r8 == 0, te, jnp.where(r8 == 1, gend, jnp.where(r8 == 2, n_used, 0.0)))
    meta_ref[...] = meta.astype(I32)


def _positions(idx2, rank, cnt, n_tiles_pad):
    n = idx2.shape[0]
    return pl.pallas_call(
        functools.partial(_pos_kernel, n_tiles_pad=n_tiles_pad),
        grid=(n // TR,),
        in_specs=[pl.BlockSpec((TR, LANES), lambda i: (i, 0)),
                  pl.BlockSpec((TR, LANES), lambda i: (i, 0)),
                  pl.BlockSpec((SUBLANES, LANES), lambda i: (0, 0))],
        out_specs=[pl.BlockSpec((TR, LANES), lambda i: (i, 0)),
                   pl.BlockSpec((SUBLANES, n_tiles_pad), lambda i: (0, 0))],
        out_shape=[jax.ShapeDtypeStruct((n, LANES), I32),
                   jax.ShapeDtypeStruct((SUBLANES, n_tiles_pad), I32)],
        compiler_params=_cparams(("arbitrary",)),
        name="moe_positions",
    )(idx2, rank, cnt)


def _row_copy(src_ref, src_row, dst_ref, dst_row, sem):
    return pltpu.make_async_copy(src_ref.at[pl.ds(src_row, 1)], dst_ref.at[pl.ds(dst_row, 1)], sem)


def _dispatch_kernel(gend_ref, pos_ref, h_ref, xs_ref, zero_ref, sem, zsem):
    def zero_copy(e):
        ge = gend_ref[e]
        return pltpu.make_async_copy(zero_ref, xs_ref.at[pl.ds(pl.multiple_of(ge - TM, TM), TM)], zsem)

    def nonempty(e):
        prev = gend_ref[jnp.maximum(e - 1, 0)]
        return jnp.where(e == 0, gend_ref[0] > 0, gend_ref[e] > prev)

    @pl.when(pl.program_id(0) == 0)
    def _():
        zero_ref[...] = jnp.zeros_like(zero_ref)

        def start(e, c):
            @pl.when(nonempty(e))
            def _():
                zero_copy(e).start()
            return c

        def wait(e, c):
            @pl.when(nonempty(e))
            def _():
                zero_copy(e).wait()
            return c

        lax.fori_loop(0, N_EXPERTS, start, 0)
        lax.fori_loop(0, N_EXPERTS, wait, 0)

        def tail_copy(i):
            return pltpu.make_async_copy(zero_ref, xs_ref.at[pl.ds(pl.multiple_of(i * TM, TM), TM)], zsem)

        def tail_start(i, c):
            tail_copy(i).start()
            return c

        def tail_wait(i, c):
            tail_copy(i).wait()
            return c

        n_used = gend_ref[N_EXPERTS - 1] // TM
        lax.fori_loop(n_used, xs_ref.shape[0] // TM, tail_start, 0)
        lax.fori_loop(n_used, xs_ref.shape[0] // TM, tail_wait, 0)

    def issue(t, c):
        for k in range(TOP_K):
            _row_copy(h_ref, t, xs_ref, pos_ref[0, t * TOP_K + k], sem).start()
        return c

    def drain(t, c):
        for k in range(TOP_K):
            _row_copy(h_ref, 0, xs_ref, 0, sem).wait()
        return c

    lax.fori_loop(0, TR, issue, 0)
    lax.fori_loop(0, TR, drain, 0)


def _dispatch(gend, pos3, hp2, n_slots):
    n, w = hp2.shape
    grid_spec = pltpu.PrefetchScalarGridSpec(
        num_scalar_prefetch=1,
        grid=(n // TR,),
        in_specs=[pl.BlockSpec((None, 1, TR * TOP_K), lambda i, g: (i, 0, 0), memory_space=pltpu.SMEM),
                  pl.BlockSpec((TR, w), lambda i, g: (i, 0))],
        out_specs=pl.BlockSpec(memory_space=pl.ANY),
        scratch_shapes=[pltpu.VMEM((TM, w), U32), pltpu.SemaphoreType.DMA, pltpu.SemaphoreType.DMA],
    )
    return pl.pallas_call(
        _dispatch_kernel,
        grid_spec=grid_spec,
        out_shape=jax.ShapeDtypeStruct((n_slots, w), U32),
        compiler_params=_cparams(("arbitrary",)),
        name="moe_dispatch",
    )(gend, pos3, hp2)


def _expert_kernel(te_ref, nu_ref, xs_ref, wgu_ref, bgu_ref, wdn_ref, bdn_ref, ys_ref, wgu_b, wdn_b):
    i = pl.program_id(0)
    prev = te_ref[jnp.maximum(i - 1, 0)]
    new_expert = (i == 0) | (te_ref[i] != prev)

    @pl.when(i < nu_ref[0])
    def _():
        @pl.when(new_expert)
        def _():
            wgu_b[...] = wgu_ref[...].astype(BF16)
            wdn_b[...] = wdn_ref[...].astype(BF16)

        wd = xs_ref[...]
        x_hi = pltpu.bitcast(wd & jnp.uint32(0xFFFF0000), F32).astype(BF16)
        x_lo = pltpu.bitcast(wd << 16, F32).astype(BF16)
        x = jnp.concatenate([x_hi, x_lo], axis=1)
        gu = jnp.dot(x, wgu_b[...], preferred_element_type=F32) + bgu_ref[...]
        gate = jnp.minimum(gu[:, :D_FF], SWIGLU_LIMIT)
        up = jnp.clip(gu[:, D_FF:], -SWIGLU_LIMIT, SWIGLU_LIMIT)
        act = (up + 1.0) * (gate * _sigmoid(SWIGLU_ALPHA * gate))
        ys_ref[...] = jnp.dot(act.astype(BF16), wdn_b[...], preferred_element_type=F32) + bdn_ref[...]

    @pl.when(i >= nu_ref[0])
    def _():
        ys_ref[...] = jnp.zeros_like(ys_ref)


def _experts(te, nu, xs, wgu, bgu, wdn, bdn):
    n_slots, w = xs.shape
    n_e, d, ff2 = wgu.shape
    used = lambda i, te_, nu_: jnp.minimum(i, nu_[0] - 1)
    grid_spec = pltpu.PrefetchScalarGridSpec(
        num_scalar_prefetch=2,
        grid=(n_slots // TM,),
        in_specs=[pl.BlockSpec((TM, w), lambda i, te_, nu_: (used(i, te_, nu_), 0)),
                  pl.BlockSpec((None, d, ff2), lambda i, te_, nu_: (te_[i], 0, 0)),
                  pl.BlockSpec((None, 1, ff2), lambda i, te_, nu_: (te_[i], 0, 0)),
                  pl.BlockSpec((None, ff2 // 2, d), lambda i, te_, nu_: (te_[i], 0, 0)),
                  pl.BlockSpec((None, 1, d), lambda i, te_, nu_: (te_[i], 0, 0))],
        out_specs=pl.BlockSpec((TM, d), lambda i, te_, nu_: (i, 0)),
        scratch_shapes=[pltpu.VMEM((d, ff2), BF16), pltpu.VMEM((ff2 // 2, d), BF16)],
    )
    return pl.pallas_call(
        _expert_kernel,
        grid_spec=grid_spec,
        out_shape=jax.ShapeDtypeStruct((n_slots, d), F32),
        compiler_params=_cparams(("arbitrary",), VMEM_LIMIT),
        name="moe_experts",
    )(te, nu, xs, wgu, bgu.reshape(n_e, 1, ff2), wdn, bdn.reshape(n_e, 1, d))


def _combine_kernel(pos_ref, x_ref, mod_ref, wts_ref, ys_ref, o_ref, buf_ref, sem):
    def issue(t, c):
        for k in range(TOP_K):
            _row_copy(ys_ref, pos_ref[0, t * TOP_K + k], buf_ref.at[k], t, sem).start()
        return c

    def drain(t, c):
        for k in range(TOP_K):
            _row_copy(ys_ref, 0, buf_ref.at[k], 0, sem).wait()
        return c

    lax.fori_loop(0, TR, issue, 0)
    lax.fori_loop(0, TR, drain, 0)
    wts = wts_ref[...]
    y = wts[:, 0:1] * buf_ref[0]
    for k in range(1, TOP_K):
        y = y + wts[:, k:k + 1] * buf_ref[k]
    o_ref[...] = x_ref[...] + mod_ref[5:6, :] * y


def _combine(pos3, xa, mod_l, wts, ys):
    b, t, d = xa.shape
    nt = t // TR
    return pl.pallas_call(
        _combine_kernel,
        grid=(b, nt),
        in_specs=[pl.BlockSpec((None, 1, TR * TOP_K), lambda bi, ti: (bi * nt + ti, 0, 0),
                               memory_space=pltpu.SMEM),
                  pl.BlockSpec((None, TR, d), lambda bi, ti: (bi, ti, 0)),
                  pl.BlockSpec((None, N_MOD, d), lambda bi, ti: (jnp.where(ti == 0, b, bi), 0, 0)),
                  pl.BlockSpec((None, TR, LANES), lambda bi, ti: (bi, ti, 0)),
                  pl.BlockSpec(memory_space=pl.ANY)],
        out_specs=pl.BlockSpec((None, TR, d), lambda bi, ti: (bi, ti, 0)),
        out_shape=jax.ShapeDtypeStruct((b, t, d), F32),
        scratch_shapes=[pltpu.VMEM((TOP_K, TR, d), F32), pltpu.SemaphoreType.DMA],
        compiler_params=_cparams(("arbitrary", "arbitrary"), VMEM_LIMIT),
        name="moe_combine",
    )(pos3, xa, mod_l, wts, ys)


def _final_kernel(x_ref, g_ref, o_ref):
    x = x_ref[...]
    o_ref[...] = x * lax.rsqrt(jnp.mean(x * x, axis=-1, keepdims=True) + EPS) * g_ref[...]


def _final_norm(xa, g, n_ctx):
    b, t, d = xa.shape
    skip = n_ctx // TR
    return pl.pallas_call(
        _final_kernel,
        grid=(b, (t - n_ctx) // TR),
        in_specs=[pl.BlockSpec((None, TR, d), lambda bi, ti: (bi, ti + skip, 0)),
                  pl.BlockSpec((1, d), lambda bi, ti: (0, 0))],
        out_specs=pl.BlockSpec((None, TR, d), lambda bi, ti: (bi, ti, 0)),
        out_shape=jax.ShapeDtypeStruct((b, t - n_ctx, d), F32),
        compiler_params=_cparams(("parallel", "parallel")),
        name="final_norm",
    )(xa, g.reshape(1, d))


def _rope_tables(n_ctx, seq):
    rows = seq // GRID_W
    r, col = jnp.meshgrid(jnp.arange(rows, dtype=F32), jnp.arange(GRID_W, dtype=F32), indexing='ij')
    inv = ROPE_THETA ** (-jnp.arange(ROPE_PAIRS, dtype=F32) / ROPE_PAIRS)
    ang_r = r.reshape(-1)[:, None] * inv
    ang_c = col.reshape(-1)[:, None] * inv
    cos_h = jnp.concatenate([jnp.cos(ang_r)] * 2 + [jnp.cos(ang_c)] * 2, axis=1)
    sin_h = jnp.concatenate([-jnp.sin(ang_r), jnp.sin(ang_r), -jnp.sin(ang_c), jnp.sin(ang_c)], axis=1)
    cos_t = jnp.concatenate([jnp.ones((n_ctx, ATT_DH), F32), cos_h], axis=0)
    sin_t = jnp.concatenate([jnp.zeros((n_ctx, ATT_DH), F32), sin_h], axis=0)
    reps = LANES // ATT_DH
    return jnp.tile(cos_t, (1, reps)), jnp.tile(sin_t, (1, reps))


def _small_weight(w_in_l, which):
    d = w_in_l.shape[0]
    off_beta = ATT_W + DN_W
    off_g = off_beta + 2 * DN_H
    cols = []
    for h in range(DN_H):
        src = [off_beta + h, off_beta + DN_H + h, off_g + h, off_g + DN_H + h]
        cols.append(jnp.concatenate([w_in_l[:, s:s + 1] for s in src]
                                    + [jnp.zeros((d, LANES - 4), w_in_l.dtype)], axis=1))
    return jnp.concatenate(cols, axis=1)


def _small_vec(v2):
    cols = []
    for h in range(DN_H):
        cols.append(jnp.concatenate([jnp.zeros((2,), F32), v2[0:1, h], v2[1:2, h],
                                     jnp.zeros((LANES - 4,), F32)]))
    return jnp.concatenate(cols).reshape(1, SM_W)


def kernel(x, c, ctx, c_ctx, w_mod, b_mod, norm1_g, w_in, q_norm_g, k_norm_g, dn_conv_w, dn_a_log,
           dn_dt_bias, dn_norm_g, conf_dw_w, conf_dw_b, conf_ln_g, conf_ln_b, w_att_o, w_dn_o, w_conf_o,
           w_out, norm2_g, w_router, b_router, w_gate_up, b_gate_up, w_down, b_down, final_g):
    b, seq, d = x.shape
    n_ctx = ctx.shape[1]
    depth = w_mod.shape[0]
    assert d == D_MODEL and n_ctx == TR and seq % TR == 0 and seq % GRID_W == 0
    t = n_ctx + seq
    n_tok = b * t
    n_slots = n_tok * TOP_K + N_EXPERTS * TM
    n_tiles_pad = -(-(n_slots // TM) // LANES) * LANES

    xa = jnp.concatenate([ctx, x], axis=1)
    mod_rows = -(-(b + 1) // SUBLANES) * SUBLANES
    cc = jnp.concatenate([c, c_ctx[None, :], jnp.zeros((mod_rows - b - 1, d), F32)], axis=0)
    mod = _mod_all(cc, w_mod, b_mod).reshape(depth, mod_rows, N_MOD, d)

    cos_t, sin_t = _rope_tables(n_ctx, seq)
    hblk = _iota((LANES, LANES), 0) // ATT_DH == _iota((LANES, LANES), 1) // ATT_DH
    bavg = jnp.where(hblk, 1.0 / ATT_DH, 0.0).astype(BF16)

    o_dn = ATT_W
    o_gate = ATT_W + DN_W + 4 * DN_H
    o_conf = o_gate + DN_H * DN_DV
    o_br = o_conf + 2 * CONF_CH

    for l in range(depth):
        w = w_in[l]
        mod_l = mod[l]
        g1 = norm1_g[l].reshape(1, d)
        qkg = jnp.concatenate([jnp.tile(q_norm_g[l], ATT_HQ), jnp.tile(k_norm_g[l], ATT_HKV)]).reshape(1, ATT_QK)
        att, dnqkv, sm, gate, glu = _inproj(
            xa, mod_l, g1, w[:, :ATT_W].astype(BF16), w[:, o_dn:o_dn + DN_W].astype(BF16),
            _small_weight(w, None).astype(BF16), w[:, o_gate:o_conf].astype(BF16),
            w[:, o_conf:o_br].astype(BF16), qkg, cos_t, sin_t, bavg,
            _small_vec(dn_a_log[l]), _small_vec(dn_dt_bias[l]))

        att_o = _attention(att)
        u, wv, qg, kd, intra, egl = _dnprep(dnqkv, dn_conv_w[l], sm, n_ctx)
        o_f, o_b = _dnscan(u, wv, qg, kd, intra, egl)
        conf_u = _conformer(glu, conf_dw_w[l], conf_dw_b[l].reshape(1, CONF_CH),
                            conf_ln_g[l].reshape(1, CONF_CH), conf_ln_b[l].reshape(1, CONF_CH), n_ctx)

        wr = jnp.concatenate([w_router[l], jnp.zeros((d, LANES - N_EXPERTS), F32)], axis=1)
        br = jnp.concatenate([b_router[l], jnp.zeros((LANES - N_EXPERTS,), F32)]).reshape(1, LANES)
        x1, hp, idx, wts = _merge(
            xa, mod_l, g1, norm2_g[l].reshape(1, d), att_o, o_f, o_b, gate, conf_u,
            w[:, o_br:].astype(BF16), w_att_o[l].astype(BF16), w_dn_o[l].astype(BF16),
            w_conf_o[l].astype(BF16), w_out[l].astype(BF16), dn_norm_g[l].reshape(1, DN_DV), wr, br)

        idx2 = idx.reshape(n_tok, LANES)
        rank, cnt = _rank(idx2)
        pos, meta = _positions(idx2, rank, cnt, n_tiles_pad)
        pos3 = pos[:, :TOP_K].reshape(n_tok // TR, 1, TR * TOP_K)
        te = meta[0, :n_slots // TM]
        gend = meta[1, :N_EXPERTS]
        nu = meta[2, :1]
        xs = _dispatch(gend, pos3, hp.reshape(n_tok, d // 2), n_slots)
        ys = _experts(te, nu, xs, w_gate_up[l], b_gate_up[l], w_down[l], b_down[l])
        xa = _combine(pos3, x1, mod_l, wts, ys)

    return _final_norm(xa, final_g, n_ctx)
```

```python
import functools

import jax
import jax.numpy as jnp
from jax import lax
from jax.experimental import pallas as pl
from jax.experimental.pallas import tpu as pltpu

F32 = jnp.float32
BF16 = jnp.bfloat16
I32 = jnp.int32
U32 = jnp.uint32

D_MODEL = 1024
N_MOD = 6
EPS = 1e-6
GRID_W = 64
ATT_HQ, ATT_HKV, ATT_DH = 8, 2, 64
ATT_GROUP = ATT_HQ // ATT_HKV
ROPE_THETA = 10000.0
ROPE_PAIRS = ATT_DH // 4
DN_H, DN_DK, DN_DV, DN_CONV, DN_CHUNK = 4, 128, 128, 4, 64
CONF_CH, CONF_KW = 512, 31
N_EXPERTS, TOP_K, D_FF = 32, 4, 1024
SWIGLU_LIMIT, SWIGLU_ALPHA = 7.0, 1.702
N_BRANCH = 3

LANES = 128
SUBLANES = 8
TR = 256
TM = 512
DN_GROUP = TR // DN_CHUNK
SCAN_STATE_BATCH = 4
CONF_TR = 128
CONF_HALO = 16
ATT_W = ATT_HQ * ATT_DH + 2 * ATT_HKV * ATT_DH
ATT_QK = ATT_HQ * ATT_DH + ATT_HKV * ATT_DH
DN_W = 3 * DN_H * DN_DK
SM_W = DN_H * LANES
VMEM_LIMIT = 52 * 1024 * 1024


def _cparams(sem, vmem=None):
    return pltpu.CompilerParams(dimension_semantics=sem, vmem_limit_bytes=vmem)


def _dot(a, b):
    return jnp.dot(a.astype(BF16), b.astype(BF16), preferred_element_type=F32)


def _dot_nt(a, b):
    return lax.dot_general(a.astype(BF16), b.astype(BF16), (((1,), (1,)), ((), ())),
                           preferred_element_type=F32)


def _dot_tn(a, b):
    return lax.dot_general(a.astype(BF16), b.astype(BF16), (((0,), (0,)), ((), ())),
                           preferred_element_type=F32)


def _split2(x):
    hi = x.astype(BF16)
    lo = (x - hi.astype(F32)).astype(BF16)
    return hi, lo


def _split3(x):
    hi = x.astype(BF16)
    r = x - hi.astype(F32)
    mid = r.astype(BF16)
    lo = (r - mid.astype(F32)).astype(BF16)
    return hi, mid, lo


def _dot_sel_l(sel3, x):
    return jnp.dot(sel3, jnp.concatenate(_split3(x), axis=0), preferred_element_type=F32)


def _dot_sel_r(x, sel):
    return jnp.dot(jnp.concatenate(_split3(x), axis=1), jnp.concatenate([sel] * 3, axis=0),
                   preferred_element_type=F32)


def _dot_hl(a, b):
    ah, al = _split2(a)
    bh, bl = _split2(b)
    return jnp.dot(jnp.concatenate([ah, ah, al], axis=1), jnp.concatenate([bh, bl, bh], axis=0),
                   preferred_element_type=F32)


def _sigmoid(x):
    return jax.nn.sigmoid(x)


def _silu(x):
    return x * jax.nn.sigmoid(x)


def _softplus(x):
    return jnp.maximum(x, 0.0) + jnp.log1p(jnp.exp(-jnp.abs(x)))


def _rms_mod(x, g, shift, scale):
    ms = jnp.mean(x * x, axis=-1, keepdims=True)
    y = x * lax.rsqrt(ms + EPS) * g
    return y * (1.0 + scale) + shift


def _iota(shape, dim):
    return lax.broadcasted_iota(I32, shape, dim)


def _mod_kernel(cc_ref, w_ref, b_ref, o_ref):
    o_ref[...] = _dot_hl(_silu(cc_ref[...]), w_ref[...]) + b_ref[...]


def _mod_all(cc, w_mod, b_mod):
    n_l, d, n = w_mod.shape
    tn = 1536
    rows = cc.shape[0]
    return pl.pallas_call(
        _mod_kernel,
        grid=(n_l, n // tn),
        in_specs=[pl.BlockSpec((rows, d), lambda l, j: (0, 0)),
                  pl.BlockSpec((None, d, tn), lambda l, j: (l, 0, j)),
                  pl.BlockSpec((None, 1, tn), lambda l, j: (l, 0, j))],
        out_specs=pl.BlockSpec((None, rows, tn), lambda l, j: (l, 0, j)),
        out_shape=jax.ShapeDtypeStruct((n_l, rows, n), F32),
        compiler_params=_cparams(("parallel", "parallel"), VMEM_LIMIT),
        name="mod_all",
    )(cc, w_mod, b_mod.reshape(n_l, 1, n))


def _inproj_kernel(x_ref, mod_ref, g1_ref, watt_ref, wdn_ref, wsm_ref, wgate_ref, wconf_ref,
                   qkg_ref, cos_ref, sin_ref, bavg_ref, alog_ref, dtb_ref,
                   att_ref, dnqkv_ref, sm_ref, gate_ref, glu_ref):
    h = _rms_mod(x_ref[...], g1_ref[...], mod_ref[0:1, :], mod_ref[1:2, :]).astype(BF16)

    za = jnp.dot(h, watt_ref[...], preferred_element_type=F32)
    cs = cos_ref[...]
    sn = sin_ref[...]
    bavg = bavg_ref[...]
    first = (_iota((TR, LANES), 1) & 31) < ROPE_PAIRS
    n_q_blocks = ATT_HQ * ATT_DH // LANES
    for j in range(ATT_QK // LANES):
        sl = slice(LANES * j, LANES * (j + 1))
        blk = za[:, sl]
        hi, lo = _split2(blk * blk)
        ms = jnp.dot(hi, bavg, preferred_element_type=F32) + jnp.dot(lo, bavg, preferred_element_type=F32)
        y = blk * lax.rsqrt(ms + EPS) * qkg_ref[:, sl]
        rot = jnp.where(first, pltpu.roll(y, LANES - ROPE_PAIRS, 1), pltpu.roll(y, ROPE_PAIRS, 1))
        y = y * cs + rot * sn
        if j < n_q_blocks:
            y = y * (ATT_DH ** -0.5)
        att_ref[:, sl] = y.astype(BF16)
    att_ref[:, ATT_QK:ATT_W] = za[:, ATT_QK:ATT_W].astype(BF16)

    dnqkv_ref[...] = jnp.dot(h, wdn_ref[...], preferred_element_type=F32)

    zs = jnp.dot(h, wsm_ref[...], preferred_element_type=F32)
    ln = _iota(zs.shape, 1) & (LANES - 1)
    g = -jnp.exp(alog_ref[...]) * _softplus(zs + dtb_ref[...])
    sm_ref[...] = jnp.where(ln < 2, _sigmoid(zs), jnp.where(ln < 4, g, 0.0))

    gate_ref[...] = jnp.dot(h, wgate_ref[...], preferred_element_type=F32)

    zc = jnp.dot(h, wconf_ref[...], preferred_element_type=F32)
    glu_ref[...] = zc[:, :CONF_CH] * _sigmoid(zc[:, CONF_CH:])


def _inproj(xa, mod_l, g1, watt, wdn, wsm, wgate, wconf, qkg, cos_t, sin_t, bavg, alog, dtb):
    b, t, d = xa.shape
    nt = t // TR
    row = lambda bi, ti: (bi, ti, 0)
    const = lambda bi, ti: (0, 0)
    full = lambda a: pl.BlockSpec(a.shape, const)
    return pl.pallas_call(
        _inproj_kernel,
        grid=(b, nt),
        in_specs=[pl.BlockSpec((None, TR, d), row),
                  pl.BlockSpec((None, N_MOD, d), lambda bi, ti: (jnp.where(ti == 0, b, bi), 0, 0)),
                  full(g1), full(watt), full(wdn), full(wsm), full(wgate), full(wconf), full(qkg),
                  pl.BlockSpec((TR, LANES), lambda bi, ti: (ti, 0)),
                  pl.BlockSpec((TR, LANES), lambda bi, ti: (ti, 0)),
                  full(bavg), full(alog), full(dtb)],
        out_specs=[pl.BlockSpec((None, TR, ATT_W), row),
                   pl.BlockSpec((None, TR, DN_W), row),
                   pl.BlockSpec((None, TR, SM_W), row),
                   pl.BlockSpec((None, TR, DN_H * DN_DV), row),
                   pl.BlockSpec((None, TR, CONF_CH), row)],
        out_shape=[jax.ShapeDtypeStruct((b, t, ATT_W), BF16),
                   jax.ShapeDtypeStruct((b, t, DN_W), F32),
                   jax.ShapeDtypeStruct((b, t, SM_W), F32),
                   jax.ShapeDtypeStruct((b, t, DN_H * DN_DV), F32),
                   jax.ShapeDtypeStruct((b, t, CONF_CH), F32)],
        compiler_params=_cparams(("parallel", "parallel"), VMEM_LIMIT),
        name="inproj",
    )(xa, mod_l, g1, watt, wdn, wsm, wgate, wconf, qkg, cos_t, sin_t, bavg, alog, dtb)


def _attn_kernel(q_ref, kv_ref, o_ref, *, n_keys):
    def run(nk):
        for kvh in range(ATT_HKV):
            k = kv_ref[0:nk, ATT_HQ * ATT_DH + ATT_DH * kvh:ATT_HQ * ATT_DH + ATT_DH * (kvh + 1)]
            v = kv_ref[0:nk, ATT_QK + ATT_DH * kvh:ATT_QK + ATT_DH * (kvh + 1)]
            for gi in range(ATT_GROUP):
                hq = kvh * ATT_GROUP + gi
                q = q_ref[:, ATT_DH * hq:ATT_DH * (hq + 1)]
                s = lax.dot_general(q, k, (((1,), (1,)), ((), ())), preferred_element_type=F32)
                p = jnp.exp(s - jnp.max(s, axis=-1, keepdims=True))
                l = jnp.sum(p, axis=-1, keepdims=True)
                o = jnp.dot(p.astype(BF16), v, preferred_element_type=F32) / l
                o_ref[:, ATT_DH * hq:ATT_DH * (hq + 1)] = o.astype(BF16)

    @pl.when(pl.program_id(1) == 0)
    def _():
        run(TR)

    @pl.when(pl.program_id(1) != 0)
    def _():
        run(n_keys)


def _attention(att):
    b, t, _ = att.shape
    nt = t // TR
    return pl.pallas_call(
        functools.partial(_attn_kernel, n_keys=t),
        grid=(b, nt),
        in_specs=[pl.BlockSpec((None, TR, ATT_W), lambda bi, ti: (bi, ti, 0)),
                  pl.BlockSpec((None, t, ATT_W), lambda bi, ti: (bi, 0, 0))],
        out_specs=pl.BlockSpec((None, TR, ATT_HQ * ATT_DH), lambda bi, ti: (bi, ti, 0)),
        out_shape=jax.ShapeDtypeStruct((b, t, ATT_HQ * ATT_DH), BF16),
        compiler_params=_cparams(("parallel", "parallel"), VMEM_LIMIT),
        name="attention",
    )(att, att)


def _dnprep_kernel(q_ref, k_ref, v_ref, cwq_ref, cwk_ref, cwv_ref, sm_ref,
                   u_ref, wq_ref, kd_ref, in_ref, egl_ref,
                   qs_ref, ks_ref, vs_ref, *, n_ctx):
    t = q_ref.shape[0]
    row = _iota((t, LANES), 0)
    seg_first = (row == 0) | (row == n_ctx)
    seg_last = (row == n_ctx - 1) | (row == t - 1)
    seg_last2 = seg_last | (row == n_ctx - 2) | (row == t - 2)

    def conv_silu(x_ref, w_ref_):
        x = x_ref[...]
        xm1 = jnp.where(seg_first, 0.0, pltpu.roll(x, 1, 0))
        xp1 = jnp.where(seg_last, 0.0, pltpu.roll(x, t - 1, 0))
        xp2 = jnp.where(seg_last2, 0.0, pltpu.roll(x, t - 2, 0))
        y = w_ref_[0:1, :] * xm1 + w_ref_[1:2, :] * x + w_ref_[2:3, :] * xp1 + w_ref_[3:4, :] * xp2
        return _silu(y)

    def l2n(x):
        return x * lax.rsqrt(jnp.sum(x * x, axis=-1, keepdims=True) + EPS)

    qs_ref[...] = l2n(conv_silu(q_ref, cwq_ref)) * (DN_DK ** -0.5)
    ks_ref[...] = l2n(conv_silu(k_ref, cwk_ref))
    vs_ref[...] = conv_silu(v_ref, cwv_ref)

    c = DN_CHUNK
    wd = c * DN_GROUP
    ri = _iota((c, wd), 0)
    cj = _iota((c, wd), 1) & (c - 1)
    lower_w = ri >= cj
    upper_w = ri <= cj
    eye_w = jnp.where(ri == cj, 1.0, 0.0).astype(F32)
    same_blk = (_iota((wd, wd), 0) // c) == (_iota((wd, wd), 1) // c)
    blk_b = jnp.where(same_blk, 1.0, 0.0).astype(BF16)
    r3 = _iota((c, 3 * c), 0)
    c3 = _iota((c, 3 * c), 1) & (c - 1)
    sel_lower = jnp.where(r3 >= c3, 1.0, 0.0).astype(BF16)
    sel_upper = jnp.where(r3 <= c3, 1.0, 0.0).astype(BF16)
    ones_b = jnp.ones((c, 3 * c), BF16)

    def fold(x):
        xm = jnp.where(same_blk, x, 0.0)
        out = xm[0:c]
        for s in range(1, DN_GROUP):
            out = out + xm[s * c:(s + 1) * c]
        return out

    def bdiag(xw):
        return jnp.concatenate([xw] * DN_GROUP, axis=0) * blk_b

    def tall(xw):
        xt = jnp.where(same_blk, jnp.concatenate([xw] * DN_GROUP, axis=0), -jnp.inf)
        return jnp.max(xt, axis=-1, keepdims=True)

    def wdot(a_hl, b_hl):
        (ah, al), (bh, bl) = a_hl, b_hl
        bdh = bdiag(bh)
        return jnp.dot(jnp.concatenate([ah, ah, al], axis=1),
                       jnp.concatenate([bdh, bdiag(bl), bdh], axis=0), preferred_element_type=F32)

    def process(tiles):
        jobs = []
        for gi in tiles:
            r0 = gi * wd if isinstance(gi, int) else pl.multiple_of(gi * wd, wd)
            q = qs_ref[pl.ds(r0, wd), :]
            k = ks_ref[pl.ds(r0, wd), :]
            v = vs_ref[pl.ds(r0, wd), :]
            smt = sm_ref[pl.ds(r0, wd), :]
            qk = fold(_dot_nt(q, k))
            for d in (0, 1):
                incl = lower_w if d == 0 else upper_w
                incl_t = upper_w if d == 0 else lower_w
                strict = (ri > cj) if d == 0 else (ri < cj)
                beta = smt[:, d:d + 1]
                g = smt[:, 2 + d:3 + d]
                gw = fold(jnp.broadcast_to(g, (wd, wd)))
                col = _dot_sel_l(sel_lower if d == 0 else sel_upper, gw)
                rowm = _dot_sel_l(ones_b, jnp.where(incl_t, gw, 0.0))
                decay = jnp.exp(jnp.where(incl, col - rowm, -jnp.inf))
                gc = tall(col)
                gl = tall(jnp.broadcast_to(col[c - 1:c, :] if d == 0 else col[0:1, :], (c, wd)))
                kb = k * beta
                lw = jnp.where(strict, fold(_dot_nt(kb, k)) * decay, 0.0)
                jobs.append(dict(gi=gi, r0=r0, d=d, q=q, k=k, vb=v * beta, kb=kb, gc=gc, gl=gl,
                                 lw=lw, intra=jnp.where(incl, qk * decay, 0.0)))

        tms = [eye_w - j["lw"] for j in jobs]
        pws = [_split2(j["lw"]) for j in jobs]
        for _ in range(5):
            pws = [_split2(wdot(pw, pw)) for pw in pws]
            tms = [tm + wdot(_split2(tm), pw) for tm, pw in zip(tms, pws)]

        for j, tm in zip(jobs, tms):
            gi, r0, d = j["gi"], j["r0"], j["d"]
            eg = jnp.exp(j["gc"])
            uw = jnp.dot(bdiag(tm.astype(BF16)),
                         jnp.concatenate([j["vb"], j["kb"] * eg], axis=1).astype(BF16),
                         preferred_element_type=F32)
            u_ref[d, pl.ds(r0, wd), :] = uw[:, :LANES]
            wv = uw[:, LANES:].astype(BF16)
            qg = (j["q"] * eg).astype(BF16)
            egl = jnp.exp(j["gl"])
            for s in range(DN_GROUP):
                r2 = 2 * r0 + 2 * c * s
                e0 = (gi * DN_GROUP + s) * SUBLANES
                if not isinstance(gi, int):
                    r2 = pl.multiple_of(r2, 2 * c)
                    e0 = pl.multiple_of(e0, SUBLANES)
                wq_ref[d, pl.ds(r2, c), :] = wv[s * c:(s + 1) * c]
                wq_ref[d, pl.ds(r2 + c, c), :] = qg[s * c:(s + 1) * c]
                egl_ref[d, pl.ds(e0, SUBLANES), :] = jnp.broadcast_to(egl[s * c:s * c + 1], (SUBLANES, LANES))
            kd_ref[d, gi] = jnp.transpose(j["k"] * jnp.exp(j["gl"] - j["gc"])).astype(BF16)
            i0 = gi * c if isinstance(gi, int) else pl.multiple_of(gi * c, c)
            in_ref[d, pl.ds(i0, c), :] = j["intra"].astype(BF16)

    n_tiles = t // wd

    def pair(pi, carry):
        process([2 * pi, 2 * pi + 1])
        return carry

    lax.fori_loop(0, n_tiles // 2, pair, 0)
    if n_tiles % 2:
        process([n_tiles - 1])


def _dnprep(dnqkv, conv_w, sm, n_ctx):
    b, t, _ = dnqkv.shape
    nc = t // DN_CHUNK
    col = lambda off: pl.BlockSpec((None, t, LANES), lambda bi, hi: (bi, 0, off + hi))
    wcol = lambda off: pl.BlockSpec((DN_CONV, LANES), lambda bi, hi: (0, off + hi))
    big = lambda rows: pl.BlockSpec((None, None, 2, rows, LANES), lambda bi, hi: (bi, hi, 0, 0, 0))
    return pl.pallas_call(
        functools.partial(_dnprep_kernel, n_ctx=n_ctx),
        grid=(b, DN_H),
        in_specs=[col(0), col(DN_H), col(2 * DN_H), wcol(0), wcol(DN_H), wcol(2 * DN_H),
                  pl.BlockSpec((None, t, LANES), lambda bi, hi: (bi, 0, hi))],
        out_specs=[big(t), big(2 * t),
                   pl.BlockSpec((None, None, 2, t // TR, DN_DK, TR), lambda bi, hi: (bi, hi, 0, 0, 0, 0)),
                   pl.BlockSpec((None, None, 2, t // DN_GROUP, TR), lambda bi, hi: (bi, hi, 0, 0, 0)),
                   pl.BlockSpec((None, None, 2, nc * SUBLANES, LANES), lambda bi, hi: (bi, hi, 0, 0, 0))],
        out_shape=[jax.ShapeDtypeStruct((b, DN_H, 2, t, LANES), F32),
                   jax.ShapeDtypeStruct((b, DN_H, 2, 2 * t, LANES), BF16),
                   jax.ShapeDtypeStruct((b, DN_H, 2, t // TR, DN_DK, TR), BF16),
                   jax.ShapeDtypeStruct((b, DN_H, 2, t // DN_GROUP, TR), BF16),
                   jax.ShapeDtypeStruct((b, DN_H, 2, nc * SUBLANES, LANES), F32)],
        scratch_shapes=[pltpu.VMEM((t, LANES), F32)] * 3,
        compiler_params=_cparams(("parallel", "parallel"), VMEM_LIMIT),
        name="dn_prep",
    )(dnqkv, dnqkv, dnqkv, conv_w, conv_w, conv_w, sm)


def _dnscan_kernel(uf_ref, wqf_ref, kdf_ref, inf_ref, egf_ref,
                   ub_ref, wqb_ref, kdb_ref, inb_ref, egb_ref,
                   of_ref, ob_ref, s_ref):
    @pl.when(pl.program_id(1) == 0)
    def _():
        s_ref[...] = jnp.zeros_like(s_ref)

    c = DN_CHUNK
    n_cc = TR // c
    fwd = (uf_ref, wqf_ref, kdf_ref, inf_ref, egf_ref, of_ref)
    bwd = (ub_ref, wqb_ref, kdb_ref, inb_ref, egb_ref, ob_ref)

    def bdiag(blocks):
        rows = []
        for i, blk in enumerate(blocks):
            z = jnp.zeros_like(blk)
            rows.append(jnp.concatenate([blk if j == i else z for j in range(len(blocks))], axis=1))
        return jnp.concatenate(rows, axis=0)

    chains = [(h, d) for h in range(DN_H) for d in (0, 1)]
    states = [s_ref[2 * h + d] for h, d in chains]
    for cc in range(n_cc):
        chunk_of = (cc, n_cc - 1 - cc)
        ws = []
        for g0 in range(0, len(chains), SCAN_STATE_BATCH):
            grp = range(g0, g0 + SCAN_STATE_BATCH)
            lhs = jnp.concatenate(
                [(fwd if chains[i][1] == 0 else bwd)[1][
                    chains[i][0], 2 * c * chunk_of[chains[i][1]]:2 * c * (chunk_of[chains[i][1]] + 1), :]
                 for i in grp], axis=1)
            res = jnp.dot(lhs, bdiag([states[i].astype(BF16) for i in grp]), preferred_element_type=F32)
            ws += [res[:, DN_DV * j:DN_DV * (j + 1)] for j in range(SCAN_STATE_BATCH)]
        for g0 in range(0, len(chains), 2):
            lhs_parts, rhs_parts = [], []
            for i in (g0, g0 + 1):
                h, d = chains[i]
                u_r, _, kd_r, in_r, _, _ = fwd if d == 0 else bwd
                ch = chunk_of[d]
                vn = (u_r[h, ch * c:(ch + 1) * c, :] - ws[i][:c]).astype(BF16)
                pad = [jnp.zeros((c, DN_DV), BF16)] * n_cc
                pad[ch] = vn
                rhs_parts.append(jnp.concatenate(pad, axis=0))
                lhs_parts.append(jnp.concatenate([in_r[h], kd_r[h]], axis=0))
            res = jnp.dot(jnp.concatenate(lhs_parts, axis=1), bdiag(rhs_parts), preferred_element_type=F32)
            for j, i in enumerate((g0, g0 + 1)):
                h, d = chains[i]
                _, _, _, _, eg_r, o_r = fwd if d == 0 else bwd
                ch = chunk_of[d]
                o_r[ch * c:(ch + 1) * c, LANES * h:LANES * (h + 1)] = (
                    ws[i][c:] + res[:c, DN_DV * j:DN_DV * (j + 1)])
                states[i] = (states[i] * eg_r[h, SUBLANES * ch:SUBLANES * ch + 1, :]
                             + res[c:, DN_DV * j:DN_DV * (j + 1)])
    for i, (h, d) in enumerate(chains):
        s_ref[2 * h + d] = states[i]


def _dnscan(u, wq, kd, intra, egl):
    b, _, _, t, _ = u.shape
    nt = t // TR
    epr = (TR // DN_CHUNK) * SUBLANES
    tb = lambda ti: jnp.where(ti == 0, 0, nt - ti)
    big_f = lambda rows: pl.BlockSpec((None, DN_H, None, rows, LANES), lambda bi, ti: (bi, 0, 0, ti, 0))
    big_b = lambda rows: pl.BlockSpec((None, DN_H, None, rows, LANES), lambda bi, ti: (bi, 0, 1, tb(ti), 0))
    kd_f = pl.BlockSpec((None, DN_H, None, None, DN_DK, TR), lambda bi, ti: (bi, 0, 0, ti, 0, 0))
    kd_b = pl.BlockSpec((None, DN_H, None, None, DN_DK, TR), lambda bi, ti: (bi, 0, 1, tb(ti), 0, 0))
    in_f = pl.BlockSpec((None, DN_H, None, DN_CHUNK, TR), lambda bi, ti: (bi, 0, 0, ti, 0))
    in_b = pl.BlockSpec((None, DN_H, None, DN_CHUNK, TR), lambda bi, ti: (bi, 0, 1, tb(ti), 0))
    width = DN_H * DN_DV
    return pl.pallas_call(
        _dnscan_kernel,
        grid=(b, nt),
        in_specs=[big_f(TR), big_f(2 * TR), kd_f, in_f, big_f(epr),
                  big_b(TR), big_b(2 * TR), kd_b, in_b, big_b(epr)],
        out_specs=[pl.BlockSpec((None, TR, width), lambda bi, ti: (bi, ti, 0)),
                   pl.BlockSpec((None, TR, width), lambda bi, ti: (bi, tb(ti), 0))],
        out_shape=[jax.ShapeDtypeStruct((b, t, width), F32)] * 2,
        scratch_shapes=[pltpu.VMEM((2 * DN_H, DN_DK, DN_DV), F32)],
        compiler_params=_cparams(("parallel", "arbitrary"), VMEM_LIMIT),
        name="dn_scan",
    )(u, wq, kd, intra, egl, u, wq, kd, intra, egl)


def _conf_kernel(glu_ref, dww_ref, dwb_ref, lng_ref, lnb_ref, o_ref, pad_ref, u_ref, *, n_ctx):
    t = glu_ref.shape[0]
    halo = CONF_HALO
    lat0 = halo + n_ctx + 2 * halo
    zeros = lambda n: jnp.zeros((n, CONF_CH), F32)
    pad_ref[0:halo, :] = zeros(halo)
    pad_ref[halo + n_ctx:lat0, :] = zeros(2 * halo)
    pad_ref[lat0 + (t - n_ctx):lat0 + (t - n_ctx) + halo, :] = zeros(halo)
    pad_ref[halo:halo + n_ctx, :] = glu_ref[0:n_ctx, :]
    pad_ref[lat0:lat0 + (t - n_ctx), :] = glu_ref[n_ctx:t, :]

    n_ctx_tiles = n_ctx // CONF_TR
    blk_rows = CONF_TR + 2 * halo
    center = (CONF_KW - 1) // 2

    def conv_tile(i, carry):
        t0 = pl.multiple_of(i * CONF_TR, CONF_TR)
        p0 = pl.multiple_of(t0 + jnp.where(i >= n_ctx_tiles, 2 * halo, 0), SUBLANES)
        for lb in range(CONF_CH // LANES):
            sl = slice(LANES * lb, LANES * (lb + 1))
            blk = pad_ref[pl.ds(p0, blk_rows), sl]
            acc = jnp.zeros((CONF_TR, LANES), F32)
            for r in range(SUBLANES):
                rolled = blk if r == 0 else pltpu.roll(blk, blk_rows - r, 0)
                for j in range(CONF_KW):
                    off = halo + j - center
                    if off % SUBLANES != r:
                        continue
                    a8 = off - r
                    acc = acc + dww_ref[j:j + 1, sl] * rolled[a8:a8 + CONF_TR, :]
            u_ref[pl.ds(t0, CONF_TR), sl] = acc + dwb_ref[:, sl]
        return carry

    lax.fori_loop(0, t // CONF_TR, conv_tile, 0)

    def ln_tile(i, carry):
        t0 = pl.multiple_of(i * CONF_TR, CONF_TR)
        x = u_ref[pl.ds(t0, CONF_TR), :]
        mu = jnp.mean(x, axis=-1, keepdims=True)
        xc = x - mu
        var = jnp.mean(xc * xc, axis=-1, keepdims=True)
        y = xc * lax.rsqrt(var + EPS) * lng_ref[...] + lnb_ref[...]
        o_ref[pl.ds(t0, CONF_TR), :] = _silu(y).astype(BF16)
        return carry

    lax.fori_loop(0, t // CONF_TR, ln_tile, 0)


def _conformer(glu, dww, dwb, lng, lnb, n_ctx):
    b, t, ch = glu.shape
    pad_rows = t + 4 * CONF_HALO
    const = lambda bi: (0, 0)
    return pl.pallas_call(
        functools.partial(_conf_kernel, n_ctx=n_ctx),
        grid=(b,),
        in_specs=[pl.BlockSpec((None, t, ch), lambda bi: (bi, 0, 0)),
                  pl.BlockSpec(dww.shape, const), pl.BlockSpec(dwb.shape, const),
                  pl.BlockSpec(lng.shape, const), pl.BlockSpec(lnb.shape, const)],
        out_specs=pl.BlockSpec((None, t, ch), lambda bi: (bi, 0, 0)),
        out_shape=jax.ShapeDtypeStruct((b, t, ch), BF16),
        scratch_shapes=[pltpu.VMEM((pad_rows, ch), F32), pltpu.VMEM((t, ch), F32)],
        compiler_params=_cparams(("parallel",), VMEM_LIMIT),
        name="conformer",
    )(glu, dww, dwb, lng, lnb)


def _merge_kernel(x_ref, mod_ref, g1_ref, g2_ref, att_ref, of_ref, ob_ref, gate_ref, cu_ref,
                  wg_ref, wao_ref, wdo_ref, wco_ref, wout_ref, dng_ref, wr_ref, br_ref,
                  xo_ref, hp_ref, idx_ref, wts_ref):
    x = x_ref[...]
    h = _rms_mod(x, g1_ref[...], mod_ref[0:1, :], mod_ref[1:2, :]).astype(BF16)

    o = of_ref[...] + ob_ref[...]
    gate = gate_ref[...]
    dn_parts = []
    for hd in range(DN_H):
        sl = slice(LANES * hd, LANES * (hd + 1))
        oh = o[:, sl]
        oh = oh * lax.rsqrt(jnp.mean(oh * oh, axis=-1, keepdims=True) + EPS) * dng_ref[...]
        dn_parts.append((oh * _silu(gate[:, sl])).astype(BF16))
    dn_o = jnp.concatenate(dn_parts, axis=1)

    y_att = jnp.dot(att_ref[...], wao_ref[...], preferred_element_type=F32)
    y_dn = jnp.dot(dn_o, wdo_ref[...], preferred_element_type=F32)
    y_cf = jnp.dot(cu_ref[...], wco_ref[...], preferred_element_type=F32)
    d = D_MODEL
    y = (_sigmoid(jnp.dot(h, wg_ref[:, 0:d], preferred_element_type=F32)) * y_att
         + _sigmoid(jnp.dot(h, wg_ref[:, d:2 * d], preferred_element_type=F32)) * y_dn
         + _sigmoid(jnp.dot(h, wg_ref[:, 2 * d:3 * d], preferred_element_type=F32)) * y_cf)
    x1 = x + mod_ref[2:3, :] * jnp.dot(y.astype(BF16), wout_ref[...], preferred_element_type=F32)
    xo_ref[...] = x1

    h2 = _rms_mod(x1, g2_ref[...], mod_ref[3:4, :], mod_ref[4:5, :])
    hb = h2.astype(BF16).astype(F32)
    w_hi = pltpu.bitcast(hb[:, :d // 2], U32) & jnp.uint32(0xFFFF0000)
    w_lo = pltpu.bitcast(hb[:, d // 2:], U32) >> 16
    hp_ref[...] = w_hi | w_lo

    lane = _iota((TR, LANES), 1)
    lanef = lane.astype(F32)
    logits = jnp.where(lane < N_EXPERTS, _dot_hl(h2, wr_ref[...]) + br_ref[...], -jnp.inf)
    vals, ids = [], []
    for _ in range(TOP_K):
        m = jnp.max(logits, axis=-1, keepdims=True)
        am = jnp.min(jnp.where(logits == m, lanef, float(LANES)), axis=-1, keepdims=True)
        vals.append(m)
        ids.append(am)
        logits = jnp.where(lanef == am, -jnp.inf, logits)
    es = [jnp.exp(v - vals[0]) for v in vals]
    den = es[0] + es[1] + es[2] + es[3]
    idx_o = jnp.zeros((TR, LANES), F32)
    wts_o = jnp.zeros((TR, LANES), F32)
    for k in range(TOP_K):
        idx_o = jnp.where(lane == k, ids[k], idx_o)
        wts_o = jnp.where(lane == k, es[k] / den, wts_o)
    idx_ref[...] = idx_o.astype(I32)
    wts_ref[...] = wts_o


def _merge(xa, mod_l, g1, g2, att_o, o_f, o_b, gate, conf_u, wg, wao, wdo, wco, wout, dng, wr, br):
    b, t, d = xa.shape
    nt = t // TR
    row = lambda bi, ti: (bi, ti, 0)
    const = lambda bi, ti: (0, 0)
    full = lambda a: pl.BlockSpec(a.shape, const)
    rowspec = lambda w: pl.BlockSpec((None, TR, w), row)
    return pl.pallas_call(
        _merge_kernel,
        grid=(b, nt),
        in_specs=[rowspec(d),
                  pl.BlockSpec((None, N_MOD, d), lambda bi, ti: (jnp.where(ti == 0, b, bi), 0, 0)),
                  full(g1), full(g2), rowspec(ATT_HQ * ATT_DH), rowspec(DN_H * DN_DV), rowspec(DN_H * DN_DV),
                  rowspec(DN_H * DN_DV), rowspec(CONF_CH),
                  full(wg), full(wao), full(wdo), full(wco), full(wout), full(dng), full(wr), full(br)],
        out_specs=[rowspec(d), rowspec(d // 2), rowspec(LANES), rowspec(LANES)],
        out_shape=[jax.ShapeDtypeStruct((b, t, d), F32),
                   jax.ShapeDtypeStruct((b, t, d // 2), U32),
                   jax.ShapeDtypeStruct((b, t, LANES), I32),
                   jax.ShapeDtypeStruct((b, t, LANES), F32)],
        compiler_params=_cparams(("parallel", "parallel"), VMEM_LIMIT),
        name="merge_router",
    )(xa, mod_l, g1, g2, att_o, o_f, o_b, gate, conf_u, wg, wao, wdo, wco, wout, dng, wr, br)


def _rank_kernel(idx_ref, rank_ref, cnt_ref, carry_ref):
    @pl.when(pl.program_id(0) == 0)
    def _():
        carry_ref[...] = jnp.zeros_like(carry_ref)

    idx = idx_ref[...]
    lane = _iota((TR, LANES), 1)
    ohs = [idx[:, k:k + 1] == lane for k in range(TOP_K)]
    cnt = jnp.zeros((TR, LANES), F32)
    for oh in ohs:
        cnt = cnt + jnp.where(oh, 1.0, 0.0)
    strict = jnp.where(_iota((TR, TR), 1) < _iota((TR, TR), 0), 1.0, 0.0).astype(BF16)
    before = jnp.dot(strict, cnt.astype(BF16), preferred_element_type=F32) + carry_ref[0:1, :]
    rank = jnp.zeros((TR, LANES), F32)
    for k, oh in enumerate(ohs):
        rk = jnp.sum(jnp.where(oh, before, 0.0), axis=-1, keepdims=True)
        rank = jnp.where(lane == k, rk, rank)
    rank_ref[...] = rank
    tot = carry_ref[0:1, :] + jnp.sum(cnt, axis=0, keepdims=True)
    carry_ref[...] = jnp.broadcast_to(tot, carry_ref.shape)
    cnt_ref[...] = jnp.broadcast_to(tot, cnt_ref.shape)


def _rank(idx2):
    n = idx2.shape[0]
    return pl.pallas_call(
        _rank_kernel,
        grid=(n // TR,),
        in_specs=[pl.BlockSpec((TR, LANES), lambda i: (i, 0))],
        out_specs=[pl.BlockSpec((TR, LANES), lambda i: (i, 0)),
                   pl.BlockSpec((SUBLANES, LANES), lambda i: (0, 0))],
        out_shape=[jax.ShapeDtypeStruct((n, LANES), F32),
                   jax.ShapeDtypeStruct((SUBLANES, LANES), F32)],
        scratch_shapes=[pltpu.VMEM((SUBLANES, LANES), F32)],
        compiler_params=_cparams(("arbitrary",)),
        name="moe_rank",
    )(idx2)


def _pos_kernel(idx_ref, rank_ref, cnt_ref, pos_ref, meta_ref, *, n_tiles_pad):
    cnt = cnt_ref[...]
    padded = jnp.floor((cnt + float(TM - 1)) * (1.0 / TM)) * float(TM)
    incl = jnp.where(_iota((LANES, LANES), 0) <= _iota((LANES, LANES), 1), 1.0, 0.0).astype(BF16)
    end = _dot_sel_r(padded, incl)
    start = (end - padded)[0:1, :]

    idx = idx_ref[...]
    rank = rank_ref[...]
    lane = _iota((TR, LANES), 1)
    pos = jnp.zeros((TR, LANES), F32)
    for k in range(TOP_K):
        st = jnp.sum(jnp.where(idx[:, k:k + 1] == lane, start, 0.0), axis=-1, keepdims=True)
        pos = jnp.where(lane == k, st + rank[:, k:k + 1], pos)
    pos_ref[...] = pos.astype(I32)

    end_t = jnp.transpose(jnp.broadcast_to(end[0:1, :], (LANES, LANES)))
    end_w = jnp.concatenate([end_t] * (n_tiles_pad // LANES), axis=1)
    tile0 = _iota((LANES, n_tiles_pad), 1).astype(F32) * float(TM)
    valid_e = _iota((LANES, n_tiles_pad), 0) < N_EXPERTS
    te = jnp.sum(jnp.where(valid_e & (end_w <= tile0), 1.0, 0.0), axis=0, keepdims=True)
    te = jnp.minimum(te, float(N_EXPERTS - 1))
    n_used = end[0:1, N_EXPERTS - 1:N_EXPERTS] * (1.0 / TM)
    gend = jnp.concatenate([end[0:1, :]] + [jnp.zeros((1, n_tiles_pad - LANES), F32)], axis=1)
    r8 = _iota((SUBLANES, n_tiles_pad), 0)
    meta = jnp.where(r8 == 0, te, jnp.where(r8 == 1, gend, jnp.where(r8 == 2, n_used, 0.0)))
    meta_ref[...] = meta.astype(I32)


def _positions(idx2, rank, cnt, n_tiles_pad):
    n = idx2.shape[0]
    return pl.pallas_call(
        functools.partial(_pos_kernel, n_tiles_pad=n_tiles_pad),
        grid=(n // TR,),
        in_specs=[pl.BlockSpec((TR, LANES), lambda i: (i, 0)),
                  pl.BlockSpec((TR, LANES), lambda i: (i, 0)),
                  pl.BlockSpec((SUBLANES, LANES), lambda i: (0, 0))],
        out_specs=[pl.BlockSpec((TR, LANES), lambda i: (i, 0)),
                   pl.BlockSpec((SUBLANES, n_tiles_pad), lambda i: (0, 0))],
        out_shape=[jax.ShapeDtypeStruct((n, LANES), I32),
                   jax.ShapeDtypeStruct((SUBLANES, n_tiles_pad), I32)],
        compiler_params=_cparams(("arbitrary",)),
        name="moe_positions",
    )(idx2, rank, cnt)


def _row_copy(src_ref, src_row, dst_ref, dst_row, sem):
    return pltpu.make_async_copy(src_ref.at[pl.ds(src_row, 1)], dst_ref.at[pl.ds(dst_row, 1)], sem)


def _dispatch_kernel(gend_ref, pos_ref, h_ref, xs_ref, zero_ref, sem, zsem):
    def zero_copy(e):
        ge = gend_ref[e]
        return pltpu.make_async_copy(zero_ref, xs_ref.at[pl.ds(pl.multiple_of(ge - TM, TM), TM)], zsem)

    def nonempty(e):
        prev = gend_ref[jnp.maximum(e - 1, 0)]
        return jnp.where(e == 0, gend_ref[0] > 0, gend_ref[e] > prev)

    @pl.when(pl.program_id(0) == 0)
    def _():
        zero_ref[...] = jnp.zeros_like(zero_ref)

        def start(e, c):
            @pl.when(nonempty(e))
            def _():
                zero_copy(e).start()
            return c

        def wait(e, c):
            @pl.when(nonempty(e))
            def _():
                zero_copy(e).wait()
            return c

        lax.fori_loop(0, N_EXPERTS, start, 0)
        lax.fori_loop(0, N_EXPERTS, wait, 0)

        def tail_copy(i):
            return pltpu.make_async_copy(zero_ref, xs_ref.at[pl.ds(pl.multiple_of(i * TM, TM), TM)], zsem)

        def tail_start(i, c):
            tail_copy(i).start()
            return c

        def tail_wait(i, c):
            tail_copy(i).wait()
            return c

        n_used = gend_ref[N_EXPERTS - 1] // TM
        lax.fori_loop(n_used, xs_ref.shape[0] // TM, tail_start, 0)
        lax.fori_loop(n_used, xs_ref.shape[0] // TM, tail_wait, 0)

    def issue(t, c):
        for k in range(TOP_K):
            _row_copy(h_ref, t, xs_ref, pos_ref[0, t * TOP_K + k], sem).start()
        return c

    def drain(t, c):
        for k in range(TOP_K):
            _row_copy(h_ref, 0, xs_ref, 0, sem).wait()
        return c

    lax.fori_loop(0, TR, issue, 0)
    lax.fori_loop(0, TR, drain, 0)


def _dispatch(gend, pos3, hp2, n_slots):
    n, w = hp2.shape
    grid_spec = pltpu.PrefetchScalarGridSpec(
        num_scalar_prefetch=1,
        grid=(n // TR,),
        in_specs=[pl.BlockSpec((None, 1, TR * TOP_K), lambda i, g: (i, 0, 0), memory_space=pltpu.SMEM),
                  pl.BlockSpec((TR, w), lambda i, g: (i, 0))],
        out_specs=pl.BlockSpec(memory_space=pl.ANY),
        scratch_shapes=[pltpu.VMEM((TM, w), U32), pltpu.SemaphoreType.DMA, pltpu.SemaphoreType.DMA],
    )
    return pl.pallas_call(
        _dispatch_kernel,
        grid_spec=grid_spec,
        out_shape=jax.ShapeDtypeStruct((n_slots, w), U32),
        compiler_params=_cparams(("arbitrary",)),
        name="moe_dispatch",
    )(gend, pos3, hp2)


def _expert_kernel(te_ref, nu_ref, xs_ref, wgu_ref, bgu_ref, wdn_ref, bdn_ref, ys_ref, wgu_b, wdn_b):
    i = pl.program_id(0)
    prev = te_ref[jnp.maximum(i - 1, 0)]
    new_expert = (i == 0) | (te_ref[i] != prev)

    @pl.when(i < nu_ref[0])
    def _():
        @pl.when(new_expert)
        def _():
            wgu_b[...] = wgu_ref[...].astype(BF16)
            wdn_b[...] = wdn_ref[...].astype(BF16)

        wd = xs_ref[...]
        x_hi = pltpu.bitcast(wd & jnp.uint32(0xFFFF0000), F32).astype(BF16)
        x_lo = pltpu.bitcast(wd << 16, F32).astype(BF16)
        x = jnp.concatenate([x_hi, x_lo], axis=1)
        gu = jnp.dot(x, wgu_b[...], preferred_element_type=F32) + bgu_ref[...]
        gate = jnp.minimum(gu[:, :D_FF], SWIGLU_LIMIT)
        up = jnp.clip(gu[:, D_FF:], -SWIGLU_LIMIT, SWIGLU_LIMIT)
        act = (up + 1.0) * (gate * _sigmoid(SWIGLU_ALPHA * gate))
        ys_ref[...] = jnp.dot(act.astype(BF16), wdn_b[...], preferred_element_type=F32) + bdn_ref[...]

    @pl.when(i >= nu_ref[0])
    def _():
        ys_ref[...] = jnp.zeros_like(ys_ref)


def _experts(te, nu, xs, layer, wgu, bgu, wdn, bdn):
    n_slots, w = xs.shape
    n_l, n_e, d, ff2 = wgu.shape
    used = lambda i, te_, nu_: jnp.minimum(i, nu_[0] - 1)
    grid_spec = pltpu.PrefetchScalarGridSpec(
        num_scalar_prefetch=2,
        grid=(n_slots // TM,),
        in_specs=[pl.BlockSpec((TM, w), lambda i, te_, nu_: (used(i, te_, nu_), 0)),
                  pl.BlockSpec((None, None, d, ff2), lambda i, te_, nu_: (layer, te_[i], 0, 0)),
                  pl.BlockSpec((None, None, 1, ff2), lambda i, te_, nu_: (layer, te_[i], 0, 0)),
                  pl.BlockSpec((None, None, ff2 // 2, d), lambda i, te_, nu_: (layer, te_[i], 0, 0)),
                  pl.BlockSpec((None, None, 1, d), lambda i, te_, nu_: (layer, te_[i], 0, 0))],
        out_specs=pl.BlockSpec((TM, d), lambda i, te_, nu_: (i, 0)),
        scratch_shapes=[pltpu.VMEM((d, ff2), BF16), pltpu.VMEM((ff2 // 2, d), BF16)],
    )
    return pl.pallas_call(
        _expert_kernel,
        grid_spec=grid_spec,
        out_shape=jax.ShapeDtypeStruct((n_slots, d), F32),
        compiler_params=_cparams(("arbitrary",), VMEM_LIMIT),
        name="moe_experts",
    )(te, nu, xs, wgu, bgu.reshape(n_l, n_e, 1, ff2), wdn, bdn.reshape(n_l, n_e, 1, d))


def _combine_kernel(pos_ref, x_ref, mod_ref, wts_ref, ys_ref, o_ref, buf_ref, sem):
    def issue(t, c):
        for k in range(TOP_K):
            _row_copy(ys_ref, pos_ref[0, t * TOP_K + k], buf_ref.at[k], t, sem).start()
        return c

    def drain(t, c):
        for k in range(TOP_K):
            _row_copy(ys_ref, 0, buf_ref.at[k], 0, sem).wait()
        return c

    lax.fori_loop(0, TR, issue, 0)
    lax.fori_loop(0, TR, drain, 0)
    wts = wts_ref[...]
    y = wts[:, 0:1] * buf_ref[0]
    for k in range(1, TOP_K):
        y = y + wts[:, k:k + 1] * buf_ref[k]
    o_ref[...] = x_ref[...] + mod_ref[5:6, :] * y


def _combine(pos3, xa, mod_l, wts, ys):
    b, t, d = xa.shape
    nt = t // TR
    return pl.pallas_call(
        _combine_kernel,
        grid=(b, nt),
        in_specs=[pl.BlockSpec((None, 1, TR * TOP_K), lambda bi, ti: (bi * nt + ti, 0, 0),
                               memory_space=pltpu.SMEM),
                  pl.BlockSpec((None, TR, d), lambda bi, ti: (bi, ti, 0)),
                  pl.BlockSpec((None, N_MOD, d), lambda bi, ti: (jnp.where(ti == 0, b, bi), 0, 0)),
                  pl.BlockSpec((None, TR, LANES), lambda bi, ti: (bi, ti, 0)),
                  pl.BlockSpec(memory_space=pl.ANY)],
        out_specs=pl.BlockSpec((None, TR, d), lambda bi, ti: (bi, ti, 0)),
        out_shape=jax.ShapeDtypeStruct((b, t, d), F32),
        scratch_shapes=[pltpu.VMEM((TOP_K, TR, d), F32), pltpu.SemaphoreType.DMA],
        compiler_params=_cparams(("arbitrary", "arbitrary"), VMEM_LIMIT),
        name="moe_combine",
    )(pos3, xa, mod_l, wts, ys)


def _final_kernel(x_ref, g_ref, o_ref):
    x = x_ref[...]
    o_ref[...] = x * lax.rsqrt(jnp.mean(x * x, axis=-1, keepdims=True) + EPS) * g_ref[...]


def _final_norm(xa, g, n_ctx):
    b, t, d = xa.shape
    skip = n_ctx // TR
    return pl.pallas_call(
        _final_kernel,
        grid=(b, (t - n_ctx) // TR),
        in_specs=[pl.BlockSpec((None, TR, d), lambda bi, ti: (bi, ti + skip, 0)),
                  pl.BlockSpec((1, d), lambda bi, ti: (0, 0))],
        out_specs=pl.BlockSpec((None, TR, d), lambda bi, ti: (bi, ti, 0)),
        out_shape=jax.ShapeDtypeStruct((b, t - n_ctx, d), F32),
        compiler_params=_cparams(("parallel", "parallel")),
        name="final_norm",
    )(xa, g.reshape(1, d))


def _rope_tables(n_ctx, seq):
    rows = seq // GRID_W
    r, col = jnp.meshgrid(jnp.arange(rows, dtype=F32), jnp.arange(GRID_W, dtype=F32), indexing='ij')
    inv = ROPE_THETA ** (-jnp.arange(ROPE_PAIRS, dtype=F32) / ROPE_PAIRS)
    ang_r = r.reshape(-1)[:, None] * inv
    ang_c = col.reshape(-1)[:, None] * inv
    cos_h = jnp.concatenate([jnp.cos(ang_r)] * 2 + [jnp.cos(ang_c)] * 2, axis=1)
    sin_h = jnp.concatenate([-jnp.sin(ang_r), jnp.sin(ang_r), -jnp.sin(ang_c), jnp.sin(ang_c)], axis=1)
    cos_t = jnp.concatenate([jnp.ones((n_ctx, ATT_DH), F32), cos_h], axis=0)
    sin_t = jnp.concatenate([jnp.zeros((n_ctx, ATT_DH), F32), sin_h], axis=0)
    reps = LANES // ATT_DH
    return jnp.tile(cos_t, (1, reps)), jnp.tile(sin_t, (1, reps))


def _small_weight(w_in_l):
    d = w_in_l.shape[0]
    off_beta = ATT_W + DN_W
    off_g = off_beta + 2 * DN_H
    cols = []
    for h in range(DN_H):
        src = [off_beta + h, off_beta + DN_H + h, off_g + h, off_g + DN_H + h]
        cols.append(jnp.concatenate([w_in_l[:, s:s + 1] for s in src]
                                    + [jnp.zeros((d, LANES - 4), w_in_l.dtype)], axis=1))
    return jnp.concatenate(cols, axis=1)


def _small_vec(v2):
    cols = []
    for h in range(DN_H):
        cols.append(jnp.concatenate([jnp.zeros((2,), F32), v2[0:1, h], v2[1:2, h],
                                     jnp.zeros((LANES - 4,), F32)]))
    return jnp.concatenate(cols).reshape(1, SM_W)


def kernel(x, c, ctx, c_ctx, w_mod, b_mod, norm1_g, w_in, q_norm_g, k_norm_g, dn_conv_w, dn_a_log,
           dn_dt_bias, dn_norm_g, conf_dw_w, conf_dw_b, conf_ln_g, conf_ln_b, w_att_o, w_dn_o, w_conf_o,
           w_out, norm2_g, w_router, b_router, w_gate_up, b_gate_up, w_down, b_down, final_g):
    b, seq, d = x.shape
    n_ctx = ctx.shape[1]
    depth = w_mod.shape[0]
    assert d == D_MODEL and n_ctx == TR and seq % TR == 0 and seq % GRID_W == 0
    t = n_ctx + seq
    n_tok = b * t
    n_slots = n_tok * TOP_K + N_EXPERTS * TM
    n_tiles_pad = -(-(n_slots // TM) // LANES) * LANES

    xa = jnp.concatenate([ctx, x], axis=1)
    mod_rows = -(-(b + 1) // SUBLANES) * SUBLANES
    cc = jnp.concatenate([c, c_ctx[None, :], jnp.zeros((mod_rows - b - 1, d), F32)], axis=0)
    mod = _mod_all(cc, w_mod, b_mod).reshape(depth, mod_rows, N_MOD, d)

    cos_t, sin_t = _rope_tables(n_ctx, seq)
    hblk = _iota((LANES, LANES), 0) // ATT_DH == _iota((LANES, LANES), 1) // ATT_DH
    bavg = jnp.where(hblk, 1.0 / ATT_DH, 0.0).astype(BF16)

    o_dn = ATT_W
    o_gate = ATT_W + DN_W + 4 * DN_H
    o_conf = o_gate + DN_H * DN_DV
    o_br = o_conf + 2 * CONF_CH

    for l in range(depth):
        w = w_in[l]
        mod_l = mod[l]
        g1 = norm1_g[l].reshape(1, d)
        qkg = jnp.concatenate([jnp.tile(q_norm_g[l], ATT_HQ), jnp.tile(k_norm_g[l], ATT_HKV)]).reshape(1, ATT_QK)
        att, dnqkv, sm, gate, glu = _inproj(
            xa, mod_l, g1, w[:, :ATT_W].astype(BF16), w[:, o_dn:o_dn + DN_W].astype(BF16),
            _small_weight(w).astype(BF16), w[:, o_gate:o_conf].astype(BF16),
            w[:, o_conf:o_br].astype(BF16), qkg, cos_t, sin_t, bavg,
            _small_vec(dn_a_log[l]), _small_vec(dn_dt_bias[l]))

        att_o = _attention(att)
        u, wq, kd, intra, egl = _dnprep(dnqkv, dn_conv_w[l], sm, n_ctx)
        o_f, o_b = _dnscan(u, wq, kd, intra, egl)
        conf_u = _conformer(glu, conf_dw_w[l], conf_dw_b[l].reshape(1, CONF_CH),
                            conf_ln_g[l].reshape(1, CONF_CH), conf_ln_b[l].reshape(1, CONF_CH), n_ctx)

        wr = jnp.concatenate([w_router[l], jnp.zeros((d, LANES - N_EXPERTS), F32)], axis=1)
        br = jnp.concatenate([b_router[l], jnp.zeros((LANES - N_EXPERTS,), F32)]).reshape(1, LANES)
        x1, hp, idx, wts = _merge(
            xa, mod_l, g1, norm2_g[l].reshape(1, d), att_o, o_f, o_b, gate, conf_u,
            w[:, o_br:].astype(BF16), w_att_o[l].astype(BF16), w_dn_o[l].astype(BF16),
            w_conf_o[l].astype(BF16), w_out[l].astype(BF16), dn_norm_g[l].reshape(1, DN_DV), wr, br)

        idx2 = idx.reshape(n_tok, LANES)
        rank, cnt = _rank(idx2)
        pos, meta = _positions(idx2, rank, cnt, n_tiles_pad)
        pos3 = pos[:, :TOP_K].reshape(n_tok // TR, 1, TR * TOP_K)
        te = meta[0, :n_slots // TM]
        gend = meta[1, :N_EXPERTS]
        nu = meta[2, :1]
        xs = _dispatch(gend, pos3, hp.reshape(n_tok, d // 2), n_slots)
        ys = _experts(te, nu, xs, l, w_gate_up, b_gate_up, w_down, b_down)
        xa = _combine(pos3, x1, mod_l, wts, ys)

    return _final_norm(xa, final_g, n_ctx)
```

```python
import functools

import jax
import jax.numpy as jnp
from jax import lax
from jax.experimental import pallas as pl
from jax.experimental.pallas import tpu as pltpu

F32 = jnp.float32
BF16 = jnp.bfloat16
I32 = jnp.int32
U32 = jnp.uint32

D_MODEL = 1024
N_MOD = 6
EPS = 1e-6
GRID_W = 64
ATT_HQ, ATT_HKV, ATT_DH = 8, 2, 64
ATT_GROUP = ATT_HQ // ATT_HKV
ROPE_THETA = 10000.0
ROPE_PAIRS = ATT_DH // 4
DN_H, DN_DK, DN_DV, DN_CONV, DN_CHUNK = 4, 128, 128, 4, 64
CONF_CH, CONF_KW = 512, 31
N_EXPERTS, TOP_K, D_FF = 32, 4, 1024
SWIGLU_LIMIT, SWIGLU_ALPHA = 7.0, 1.702
N_BRANCH = 3

LANES = 128
SUBLANES = 8
TR = 256
TM = 512
DN_GROUP = TR // DN_CHUNK
SCAN_STATE_BATCH = 4
RUN_ALIGN = SUBLANES
RUN_BITS = (TR // RUN_ALIGN).bit_length()
LOC_SLOTS = TR * TOP_K + N_EXPERTS * RUN_ALIGN
CONF_TR = 128
CONF_HALO = 16
ATT_W = ATT_HQ * ATT_DH + 2 * ATT_HKV * ATT_DH
ATT_QK = ATT_HQ * ATT_DH + ATT_HKV * ATT_DH
DN_W = 3 * DN_H * DN_DK
SM_W = DN_H * LANES
VMEM_LIMIT = 52 * 1024 * 1024


def _cparams(sem, vmem=None):
    return pltpu.CompilerParams(dimension_semantics=sem, vmem_limit_bytes=vmem)


def _dot(a, b):
    return jnp.dot(a.astype(BF16), b.astype(BF16), preferred_element_type=F32)


def _dot_nt(a, b):
    return lax.dot_general(a.astype(BF16), b.astype(BF16), (((1,), (1,)), ((), ())),
                           preferred_element_type=F32)


def _dot_tn(a, b):
    return lax.dot_general(a.astype(BF16), b.astype(BF16), (((0,), (0,)), ((), ())),
                           preferred_element_type=F32)


def _split2(x):
    hi = x.astype(BF16)
    lo = (x - hi.astype(F32)).astype(BF16)
    return hi, lo


def _split3(x):
    hi = x.astype(BF16)
    r = x - hi.astype(F32)
    mid = r.astype(BF16)
    lo = (r - mid.astype(F32)).astype(BF16)
    return hi, mid, lo


def _dot_sel_l(sel3, x):
    return jnp.dot(sel3, jnp.concatenate(_split3(x), axis=0), preferred_element_type=F32)


def _dot_sel_r(x, sel):
    return jnp.dot(jnp.concatenate(_split3(x), axis=1), jnp.concatenate([sel] * 3, axis=0),
                   preferred_element_type=F32)


def _dot_hl(a, b):
    ah, al = _split2(a)
    bh, bl = _split2(b)
    return jnp.dot(jnp.concatenate([ah, ah, al], axis=1), jnp.concatenate([bh, bl, bh], axis=0),
                   preferred_element_type=F32)


def _sigmoid(x):
    return jax.nn.sigmoid(x)


def _silu(x):
    return x * jax.nn.sigmoid(x)


def _softplus(x):
    return jnp.maximum(x, 0.0) + jnp.log1p(jnp.exp(-jnp.abs(x)))


def _rms_mod(x, g, shift, scale):
    ms = jnp.mean(x * x, axis=-1, keepdims=True)
    y = x * lax.rsqrt(ms + EPS) * g
    return y * (1.0 + scale) + shift


def _iota(shape, dim):
    return lax.broadcasted_iota(I32, shape, dim)


def _mod_kernel(cc_ref, w_ref, b_ref, o_ref):
    o_ref[...] = _dot_hl(_silu(cc_ref[...]), w_ref[...]) + b_ref[...]


def _mod_all(cc, w_mod, b_mod):
    n_l, d, n = w_mod.shape
    tn = 1536
    rows = cc.shape[0]
    return pl.pallas_call(
        _mod_kernel,
        grid=(n_l, n // tn),
        in_specs=[pl.BlockSpec((rows, d), lambda l, j: (0, 0)),
                  pl.BlockSpec((None, d, tn), lambda l, j: (l, 0, j)),
                  pl.BlockSpec((None, 1, tn), lambda l, j: (l, 0, j))],
        out_specs=pl.BlockSpec((None, rows, tn), lambda l, j: (l, 0, j)),
        out_shape=jax.ShapeDtypeStruct((n_l, rows, n), F32),
        compiler_params=_cparams(("parallel", "parallel"), VMEM_LIMIT),
        name="mod_all",
    )(cc, w_mod, b_mod.reshape(n_l, 1, n))


def _inproj_kernel(x_ref, mod_ref, g1_ref, watt_ref, wdn_ref, wsm_ref, wgate_ref, wconf_ref,
                   qkg_ref, cos_ref, sin_ref, bavg_ref, alog_ref, dtb_ref,
                   att_ref, dnqkv_ref, sm_ref, gate_ref, glu_ref):
    h = _rms_mod(x_ref[...], g1_ref[...], mod_ref[0:1, :], mod_ref[1:2, :]).astype(BF16)

    za = jnp.dot(h, watt_ref[...], preferred_element_type=F32)
    cs = cos_ref[...]
    sn = sin_ref[...]
    bavg = bavg_ref[...]
    first = (_iota((TR, LANES), 1) & 31) < ROPE_PAIRS
    n_q_blocks = ATT_HQ * ATT_DH // LANES
    for j in range(ATT_QK // LANES):
        sl = slice(LANES * j, LANES * (j + 1))
        blk = za[:, sl]
        hi, lo = _split2(blk * blk)
        ms = jnp.dot(hi, bavg, preferred_element_type=F32) + jnp.dot(lo, bavg, preferred_element_type=F32)
        y = blk * lax.rsqrt(ms + EPS) * qkg_ref[:, sl]
        rot = jnp.where(first, pltpu.roll(y, LANES - ROPE_PAIRS, 1), pltpu.roll(y, ROPE_PAIRS, 1))
        y = y * cs + rot * sn
        if j < n_q_blocks:
            y = y * (ATT_DH ** -0.5)
        att_ref[:, sl] = y.astype(BF16)
    att_ref[:, ATT_QK:ATT_W] = za[:, ATT_QK:ATT_W].astype(BF16)

    dnqkv_ref[...] = jnp.dot(h, wdn_ref[...], preferred_element_type=F32)

    zs = jnp.dot(h, wsm_ref[...], preferred_element_type=F32)
    ln = _iota(zs.shape, 1) & (LANES - 1)
    g = -jnp.exp(alog_ref[...]) * _softplus(zs + dtb_ref[...])
    sm_ref[...] = jnp.where(ln < 2, _sigmoid(zs), jnp.where(ln < 4, g, 0.0))

    gate_ref[...] = jnp.dot(h, wgate_ref[...], preferred_element_type=F32)

    zc = jnp.dot(h, wconf_ref[...], preferred_element_type=F32)
    glu_ref[...] = zc[:, :CONF_CH] * _sigmoid(zc[:, CONF_CH:])


def _inproj(xa, mod_l, g1, watt, wdn, wsm, wgate, wconf, qkg, cos_t, sin_t, bavg, alog, dtb):
    b, t, d = xa.shape
    nt = t // TR
    row = lambda bi, ti: (bi, ti, 0)
    const = lambda bi, ti: (0, 0)
    full = lambda a: pl.BlockSpec(a.shape, const)
    return pl.pallas_call(
        _inproj_kernel,
        grid=(b, nt),
        in_specs=[pl.BlockSpec((None, TR, d), row),
                  pl.BlockSpec((None, N_MOD, d), lambda bi, ti: (jnp.where(ti == 0, b, bi), 0, 0)),
                  full(g1), full(watt), full(wdn), full(wsm), full(wgate), full(wconf), full(qkg),
                  pl.BlockSpec((TR, LANES), lambda bi, ti: (ti, 0)),
                  pl.BlockSpec((TR, LANES), lambda bi, ti: (ti, 0)),
                  full(bavg), full(alog), full(dtb)],
        out_specs=[pl.BlockSpec((None, TR, ATT_W), row),
                   pl.BlockSpec((None, TR, DN_W), row),
                   pl.BlockSpec((None, TR, SM_W), row),
                   pl.BlockSpec((None, TR, DN_H * DN_DV), row),
                   pl.BlockSpec((None, TR, CONF_CH), row)],
        out_shape=[jax.ShapeDtypeStruct((b, t, ATT_W), BF16),
                   jax.ShapeDtypeStruct((b, t, DN_W), F32),
                   jax.ShapeDtypeStruct((b, t, SM_W), F32),
                   jax.ShapeDtypeStruct((b, t, DN_H * DN_DV), F32),
                   jax.ShapeDtypeStruct((b, t, CONF_CH), F32)],
        compiler_params=_cparams(("parallel", "parallel"), VMEM_LIMIT),
        name="inproj",
    )(xa, mod_l, g1, watt, wdn, wsm, wgate, wconf, qkg, cos_t, sin_t, bavg, alog, dtb)


def _attn_kernel(q_ref, kv_ref, o_ref, *, n_keys):
    def run(nk):
        for kvh in range(ATT_HKV):
            k = kv_ref[0:nk, ATT_HQ * ATT_DH + ATT_DH * kvh:ATT_HQ * ATT_DH + ATT_DH * (kvh + 1)]
            v = kv_ref[0:nk, ATT_QK + ATT_DH * kvh:ATT_QK + ATT_DH * (kvh + 1)]
            for gi in range(ATT_GROUP):
                hq = kvh * ATT_GROUP + gi
                q = q_ref[:, ATT_DH * hq:ATT_DH * (hq + 1)]
                s = lax.dot_general(q, k, (((1,), (1,)), ((), ())), preferred_element_type=F32)
                p = jnp.exp(s - jnp.max(s, axis=-1, keepdims=True))
                l = jnp.sum(p, axis=-1, keepdims=True)
                o = jnp.dot(p.astype(BF16), v, preferred_element_type=F32) / l
                o_ref[:, ATT_DH * hq:ATT_DH * (hq + 1)] = o.astype(BF16)

    @pl.when(pl.program_id(1) == 0)
    def _():
        run(TR)

    @pl.when(pl.program_id(1) != 0)
    def _():
        run(n_keys)


def _attention(att):
    b, t, _ = att.shape
    nt = t // TR
    return pl.pallas_call(
        functools.partial(_attn_kernel, n_keys=t),
        grid=(b, nt),
        in_specs=[pl.BlockSpec((None, TR, ATT_W), lambda bi, ti: (bi, ti, 0)),
                  pl.BlockSpec((None, t, ATT_W), lambda bi, ti: (bi, 0, 0))],
        out_specs=pl.BlockSpec((None, TR, ATT_HQ * ATT_DH), lambda bi, ti: (bi, ti, 0)),
        out_shape=jax.ShapeDtypeStruct((b, t, ATT_HQ * ATT_DH), BF16),
        compiler_params=_cparams(("parallel", "parallel"), VMEM_LIMIT),
        name="attention",
    )(att, att)


def _dnprep_kernel(q_ref, k_ref, v_ref, cwq_ref, cwk_ref, cwv_ref, sm_ref,
                   u_ref, wq_ref, kd_ref, in_ref, egl_ref,
                   qs_ref, ks_ref, vs_ref, *, n_ctx):
    t = q_ref.shape[0]
    row = _iota((t, LANES), 0)
    seg_first = (row == 0) | (row == n_ctx)
    seg_last = (row == n_ctx - 1) | (row == t - 1)
    seg_last2 = seg_last | (row == n_ctx - 2) | (row == t - 2)

    def conv_silu(x_ref, w_ref_):
        x = x_ref[...]
        xm1 = jnp.where(seg_first, 0.0, pltpu.roll(x, 1, 0))
        xp1 = jnp.where(seg_last, 0.0, pltpu.roll(x, t - 1, 0))
        xp2 = jnp.where(seg_last2, 0.0, pltpu.roll(x, t - 2, 0))
        y = w_ref_[0:1, :] * xm1 + w_ref_[1:2, :] * x + w_ref_[2:3, :] * xp1 + w_ref_[3:4, :] * xp2
        return _silu(y)

    def l2n(x):
        return x * lax.rsqrt(jnp.sum(x * x, axis=-1, keepdims=True) + EPS)

    qs_ref[...] = l2n(conv_silu(q_ref, cwq_ref)) * (DN_DK ** -0.5)
    ks_ref[...] = l2n(conv_silu(k_ref, cwk_ref))
    vs_ref[...] = conv_silu(v_ref, cwv_ref)

    c = DN_CHUNK
    wd = c * DN_GROUP
    ri = _iota((c, wd), 0)
    cj = _iota((c, wd), 1) & (c - 1)
    lower_w = ri >= cj
    upper_w = ri <= cj
    eye_w = jnp.where(ri == cj, 1.0, 0.0).astype(F32)
    same_blk = (_iota((wd, wd), 0) // c) == (_iota((wd, wd), 1) // c)
    blk_b = jnp.where(same_blk, 1.0, 0.0).astype(BF16)
    r3 = _iota((c, 3 * c), 0)
    c3 = _iota((c, 3 * c), 1) & (c - 1)
    sel_lower = jnp.where(r3 >= c3, 1.0, 0.0).astype(BF16)
    sel_upper = jnp.where(r3 <= c3, 1.0, 0.0).astype(BF16)
    ones_b = jnp.ones((c, 3 * c), BF16)

    def fold(x):
        xm = jnp.where(same_blk, x, 0.0)
        out = xm[0:c]
        for s in range(1, DN_GROUP):
            out = out + xm[s * c:(s + 1) * c]
        return out

    def bdiag(xw):
        return jnp.concatenate([xw] * DN_GROUP, axis=0) * blk_b

    def tall(xw):
        xt = jnp.where(same_blk, jnp.concatenate([xw] * DN_GROUP, axis=0), -jnp.inf)
        return jnp.max(xt, axis=-1, keepdims=True)

    def wdot(a_hl, b_hl):
        (ah, al), (bh, bl) = a_hl, b_hl
        bdh = bdiag(bh)
        return jnp.dot(jnp.concatenate([ah, ah, al], axis=1),
                       jnp.concatenate([bdh, bdiag(bl), bdh], axis=0), preferred_element_type=F32)

    def process(tiles):
        jobs = []
        for gi in tiles:
            r0 = gi * wd if isinstance(gi, int) else pl.multiple_of(gi * wd, wd)
            q = qs_ref[pl.ds(r0, wd), :]
            k = ks_ref[pl.ds(r0, wd), :]
            v = vs_ref[pl.ds(r0, wd), :]
            smt = sm_ref[pl.ds(r0, wd), :]
            qk = fold(_dot_nt(q, k))
            for d in (0, 1):
                incl = lower_w if d == 0 else upper_w
                incl_t = upper_w if d == 0 else lower_w
                strict = (ri > cj) if d == 0 else (ri < cj)
                beta = smt[:, d:d + 1]
                g = smt[:, 2 + d:3 + d]
                gw = fold(jnp.broadcast_to(g, (wd, wd)))
                col = _dot_sel_l(sel_lower if d == 0 else sel_upper, gw)
                rowm = _dot_sel_l(ones_b, jnp.where(incl_t, gw, 0.0))
                decay = jnp.exp(jnp.where(incl, col - rowm, -jnp.inf))
                gc = tall(col)
                gl = tall(jnp.broadcast_to(col[c - 1:c, :] if d == 0 else col[0:1, :], (c, wd)))
                kb = k * beta
                lw = jnp.where(strict, fold(_dot_nt(kb, k)) * decay, 0.0)
                jobs.append(dict(gi=gi, r0=r0, d=d, q=q, k=k, vb=v * beta, kb=kb, gc=gc, gl=gl,
                                 lw=lw, intra=jnp.where(incl, qk * decay, 0.0)))

        tms = [eye_w - j["lw"] for j in jobs]
        pws = [_split2(j["lw"]) for j in jobs]
        for _ in range(5):
            pws = [_split2(wdot(pw, pw)) for pw in pws]
            tms = [tm + wdot(_split2(tm), pw) for tm, pw in zip(tms, pws)]

        for j, tm in zip(jobs, tms):
            gi, r0, d = j["gi"], j["r0"], j["d"]
            eg = jnp.exp(j["gc"])
            uw = jnp.dot(bdiag(tm.astype(BF16)),
                         jnp.concatenate([j["vb"], j["kb"] * eg], axis=1).astype(BF16),
                         preferred_element_type=F32)
            u_ref[d, pl.ds(r0, wd), :] = uw[:, :LANES]
            wv = uw[:, LANES:].astype(BF16)
            qg = (j["q"] * eg).astype(BF16)
            egl = jnp.exp(j["gl"])
            for s in range(DN_GROUP):
                r2 = 2 * r0 + 2 * c * s
                e0 = (gi * DN_GROUP + s) * SUBLANES
                if not isinstance(gi, int):
                    r2 = pl.multiple_of(r2, 2 * c)
                    e0 = pl.multiple_of(e0, SUBLANES)
                wq_ref[d, pl.ds(r2, c), :] = wv[s * c:(s + 1) * c]
                wq_ref[d, pl.ds(r2 + c, c), :] = qg[s * c:(s + 1) * c]
                egl_ref[d, pl.ds(e0, SUBLANES), :] = jnp.broadcast_to(egl[s * c:s * c + 1], (SUBLANES, LANES))
            kd_ref[d, gi] = jnp.transpose(j["k"] * jnp.exp(j["gl"] - j["gc"])).astype(BF16)
            i0 = gi * c if isinstance(gi, int) else pl.multiple_of(gi * c, c)
            in_ref[d, pl.ds(i0, c), :] = j["intra"].astype(BF16)

    n_tiles = t // wd

    def pair(pi, carry):
        process([2 * pi, 2 * pi + 1])
        return carry

    lax.fori_loop(0, n_tiles // 2, pair, 0)
    if n_tiles % 2:
        process([n_tiles - 1])


def _dnprep(dnqkv, conv_w, sm, n_ctx):
    b, t, _ = dnqkv.shape
    nc = t // DN_CHUNK
    col = lambda off: pl.BlockSpec((None, t, LANES), lambda bi, hi: (bi, 0, off + hi))
    wcol = lambda off: pl.BlockSpec((DN_CONV, LANES), lambda bi, hi: (0, off + hi))
    big = lambda rows: pl.BlockSpec((None, None, 2, rows, LANES), lambda bi, hi: (bi, hi, 0, 0, 0))
    return pl.pallas_call(
        functools.partial(_dnprep_kernel, n_ctx=n_ctx),
        grid=(b, DN_H),
        in_specs=[col(0), col(DN_H), col(2 * DN_H), wcol(0), wcol(DN_H), wcol(2 * DN_H),
                  pl.BlockSpec((None, t, LANES), lambda bi, hi: (bi, 0, hi))],
        out_specs=[big(t), big(2 * t),
                   pl.BlockSpec((None, None, 2, t // TR, DN_DK, TR), lambda bi, hi: (bi, hi, 0, 0, 0, 0)),
                   pl.BlockSpec((None, None, 2, t // DN_GROUP, TR), lambda bi, hi: (bi, hi, 0, 0, 0)),
                   pl.BlockSpec((None, None, 2, nc * SUBLANES, LANES), lambda bi, hi: (bi, hi, 0, 0, 0))],
        out_shape=[jax.ShapeDtypeStruct((b, DN_H, 2, t, LANES), F32),
                   jax.ShapeDtypeStruct((b, DN_H, 2, 2 * t, LANES), BF16),
                   jax.ShapeDtypeStruct((b, DN_H, 2, t // TR, DN_DK, TR), BF16),
                   jax.ShapeDtypeStruct((b, DN_H, 2, t // DN_GROUP, TR), BF16),
                   jax.ShapeDtypeStruct((b, DN_H, 2, nc * SUBLANES, LANES), F32)],
        scratch_shapes=[pltpu.VMEM((t, LANES), F32)] * 3,
        compiler_params=_cparams(("parallel", "parallel"), VMEM_LIMIT),
        name="dn_prep",
    )(dnqkv, dnqkv, dnqkv, conv_w, conv_w, conv_w, sm)


def _dnscan_kernel(uf_ref, wqf_ref, kdf_ref, inf_ref, egf_ref,
                   ub_ref, wqb_ref, kdb_ref, inb_ref, egb_ref,
                   of_ref, ob_ref, s_ref):
    @pl.when(pl.program_id(1) == 0)
    def _():
        s_ref[...] = jnp.zeros_like(s_ref)

    c = DN_CHUNK
    n_cc = TR // c
    fwd = (uf_ref, wqf_ref, kdf_ref, inf_ref, egf_ref, of_ref)
    bwd = (ub_ref, wqb_ref, kdb_ref, inb_ref, egb_ref, ob_ref)

    def bdiag(blocks):
        rows = []
        for i, blk in enumerate(blocks):
            z = jnp.zeros_like(blk)
            rows.append(jnp.concatenate([blk if j == i else z for j in range(len(blocks))], axis=1))
        return jnp.concatenate(rows, axis=0)

    chains = [(h, d) for h in range(DN_H) for d in (0, 1)]
    states = [s_ref[2 * h + d] for h, d in chains]
    for cc in range(n_cc):
        chunk_of = (cc, n_cc - 1 - cc)
        ws = []
        for g0 in range(0, len(chains), SCAN_STATE_BATCH):
            grp = range(g0, g0 + SCAN_STATE_BATCH)
            lhs = jnp.concatenate(
                [(fwd if chains[i][1] == 0 else bwd)[1][
                    chains[i][0], 2 * c * chunk_of[chains[i][1]]:2 * c * (chunk_of[chains[i][1]] + 1), :]
                 for i in grp], axis=1)
            res = jnp.dot(lhs, bdiag([states[i].astype(BF16) for i in grp]), preferred_element_type=F32)
            ws += [res[:, DN_DV * j:DN_DV * (j + 1)] for j in range(SCAN_STATE_BATCH)]
        for g0 in range(0, len(chains), 2):
            lhs_parts, rhs_parts = [], []
            for i in (g0, g0 + 1):
                h, d = chains[i]
                u_r, _, kd_r, in_r, _, _ = fwd if d == 0 else bwd
                ch = chunk_of[d]
                vn = (u_r[h, ch * c:(ch + 1) * c, :] - ws[i][:c]).astype(BF16)
                pad = [jnp.zeros((c, DN_DV), BF16)] * n_cc
                pad[ch] = vn
                rhs_parts.append(jnp.concatenate(pad, axis=0))
                lhs_parts.append(jnp.concatenate([in_r[h], kd_r[h]], axis=0))
            res = jnp.dot(jnp.concatenate(lhs_parts, axis=1), bdiag(rhs_parts), preferred_element_type=F32)
            for j, i in enumerate((g0, g0 + 1)):
                h, d = chains[i]
                _, _, _, _, eg_r, o_r = fwd if d == 0 else bwd
                ch = chunk_of[d]
                o_r[ch * c:(ch + 1) * c, LANES * h:LANES * (h + 1)] = (
                    ws[i][c:] + res[:c, DN_DV * j:DN_DV * (j + 1)])
                states[i] = (states[i] * eg_r[h, SUBLANES * ch:SUBLANES * ch + 1, :]
                             + res[c:, DN_DV * j:DN_DV * (j + 1)])
    for i, (h, d) in enumerate(chains):
        s_ref[2 * h + d] = states[i]


def _dnscan(u, wq, kd, intra, egl):
    b, _, _, t, _ = u.shape
    nt = t // TR
    epr = (TR // DN_CHUNK) * SUBLANES
    tb = lambda ti: jnp.where(ti == 0, 0, nt - ti)
    big_f = lambda rows: pl.BlockSpec((None, DN_H, None, rows, LANES), lambda bi, ti: (bi, 0, 0, ti, 0))
    big_b = lambda rows: pl.BlockSpec((None, DN_H, None, rows, LANES), lambda bi, ti: (bi, 0, 1, tb(ti), 0))
    kd_f = pl.BlockSpec((None, DN_H, None, None, DN_DK, TR), lambda bi, ti: (bi, 0, 0, ti, 0, 0))
    kd_b = pl.BlockSpec((None, DN_H, None, None, DN_DK, TR), lambda bi, ti: (bi, 0, 1, tb(ti), 0, 0))
    in_f = pl.BlockSpec((None, DN_H, None, DN_CHUNK, TR), lambda bi, ti: (bi, 0, 0, ti, 0))
    in_b = pl.BlockSpec((None, DN_H, None, DN_CHUNK, TR), lambda bi, ti: (bi, 0, 1, tb(ti), 0))
    width = DN_H * DN_DV
    return pl.pallas_call(
        _dnscan_kernel,
        grid=(b, nt),
        in_specs=[big_f(TR), big_f(2 * TR), kd_f, in_f, big_f(epr),
                  big_b(TR), big_b(2 * TR), kd_b, in_b, big_b(epr)],
        out_specs=[pl.BlockSpec((None, TR, width), lambda bi, ti: (bi, ti, 0)),
                   pl.BlockSpec((None, TR, width), lambda bi, ti: (bi, tb(ti), 0))],
        out_shape=[jax.ShapeDtypeStruct((b, t, width), F32)] * 2,
        scratch_shapes=[pltpu.VMEM((2 * DN_H, DN_DK, DN_DV), F32)],
        compiler_params=_cparams(("parallel", "arbitrary"), VMEM_LIMIT),
        name="dn_scan",
    )(u, wq, kd, intra, egl, u, wq, kd, intra, egl)


def _conf_kernel(glu_ref, dww_ref, dwb_ref, lng_ref, lnb_ref, o_ref, pad_ref, u_ref, *, n_ctx):
    t = glu_ref.shape[0]
    halo = CONF_HALO
    lat0 = halo + n_ctx + 2 * halo
    zeros = lambda n: jnp.zeros((n, CONF_CH), F32)
    pad_ref[0:halo, :] = zeros(halo)
    pad_ref[halo + n_ctx:lat0, :] = zeros(2 * halo)
    pad_ref[lat0 + (t - n_ctx):lat0 + (t - n_ctx) + halo, :] = zeros(halo)
    pad_ref[halo:halo + n_ctx, :] = glu_ref[0:n_ctx, :]
    pad_ref[lat0:lat0 + (t - n_ctx), :] = glu_ref[n_ctx:t, :]

    n_ctx_tiles = n_ctx // CONF_TR
    blk_rows = CONF_TR + 2 * halo
    center = (CONF_KW - 1) // 2

    def conv_tile(i, carry):
        t0 = pl.multiple_of(i * CONF_TR, CONF_TR)
        p0 = pl.multiple_of(t0 + jnp.where(i >= n_ctx_tiles, 2 * halo, 0), SUBLANES)
        for lb in range(CONF_CH // LANES):
            sl = slice(LANES * lb, LANES * (lb + 1))
            blk = pad_ref[pl.ds(p0, blk_rows), sl]
            acc = jnp.zeros((CONF_TR, LANES), F32)
            for r in range(SUBLANES):
                rolled = blk if r == 0 else pltpu.roll(blk, blk_rows - r, 0)
                for j in range(CONF_KW):
                    off = halo + j - center
                    if off % SUBLANES != r:
                        continue
                    a8 = off - r
                    acc = acc + dww_ref[j:j + 1, sl] * rolled[a8:a8 + CONF_TR, :]
            u_ref[pl.ds(t0, CONF_TR), sl] = acc + dwb_ref[:, sl]
        return carry

    lax.fori_loop(0, t // CONF_TR, conv_tile, 0)

    def ln_tile(i, carry):
        t0 = pl.multiple_of(i * CONF_TR, CONF_TR)
        x = u_ref[pl.ds(t0, CONF_TR), :]
        mu = jnp.mean(x, axis=-1, keepdims=True)
        xc = x - mu
        var = jnp.mean(xc * xc, axis=-1, keepdims=True)
        y = xc * lax.rsqrt(var + EPS) * lng_ref[...] + lnb_ref[...]
        o_ref[pl.ds(t0, CONF_TR), :] = _silu(y).astype(BF16)
        return carry

    lax.fori_loop(0, t // CONF_TR, ln_tile, 0)


def _conformer(glu, dww, dwb, lng, lnb, n_ctx):
    b, t, ch = glu.shape
    pad_rows = t + 4 * CONF_HALO
    const = lambda bi: (0, 0)
    return pl.pallas_call(
        functools.partial(_conf_kernel, n_ctx=n_ctx),
        grid=(b,),
        in_specs=[pl.BlockSpec((None, t, ch), lambda bi: (bi, 0, 0)),
                  pl.BlockSpec(dww.shape, const), pl.BlockSpec(dwb.shape, const),
                  pl.BlockSpec(lng.shape, const), pl.BlockSpec(lnb.shape, const)],
        out_specs=pl.BlockSpec((None, t, ch), lambda bi: (bi, 0, 0)),
        out_shape=jax.ShapeDtypeStruct((b, t, ch), BF16),
        scratch_shapes=[pltpu.VMEM((pad_rows, ch), F32), pltpu.VMEM((t, ch), F32)],
        compiler_params=_cparams(("parallel",), VMEM_LIMIT),
        name="conformer",
    )(glu, dww, dwb, lng, lnb)


def _merge_kernel(x_ref, mod_ref, g1_ref, g2_ref, att_ref, of_ref, ob_ref, gate_ref, cu_ref,
                  wg_ref, wao_ref, wdo_ref, wco_ref, wout_ref, dng_ref, wr_ref, br_ref,
                  xo_ref, hs_ref, lp_ref, wts_ref, cnt_ref):
    x = x_ref[...]
    h = _rms_mod(x, g1_ref[...], mod_ref[0:1, :], mod_ref[1:2, :]).astype(BF16)

    o = of_ref[...] + ob_ref[...]
    gate = gate_ref[...]
    dn_parts = []
    for hd in range(DN_H):
        sl = slice(LANES * hd, LANES * (hd + 1))
        oh = o[:, sl]
        oh = oh * lax.rsqrt(jnp.mean(oh * oh, axis=-1, keepdims=True) + EPS) * dng_ref[...]
        dn_parts.append((oh * _silu(gate[:, sl])).astype(BF16))
    dn_o = jnp.concatenate(dn_parts, axis=1)

    y_att = jnp.dot(att_ref[...], wao_ref[...], preferred_element_type=F32)
    y_dn = jnp.dot(dn_o, wdo_ref[...], preferred_element_type=F32)
    y_cf = jnp.dot(cu_ref[...], wco_ref[...], preferred_element_type=F32)
    d = D_MODEL
    y = (_sigmoid(jnp.dot(h, wg_ref[:, 0:d], preferred_element_type=F32)) * y_att
         + _sigmoid(jnp.dot(h, wg_ref[:, d:2 * d], preferred_element_type=F32)) * y_dn
         + _sigmoid(jnp.dot(h, wg_ref[:, 2 * d:3 * d], preferred_element_type=F32)) * y_cf)
    x1 = x + mod_ref[2:3, :] * jnp.dot(y.astype(BF16), wout_ref[...], preferred_element_type=F32)
    xo_ref[...] = x1

    h2 = _rms_mod(x1, g2_ref[...], mod_ref[3:4, :], mod_ref[4:5, :])

    lane = _iota((TR, LANES), 1)
    lanef = lane.astype(F32)
    logits = jnp.where(lane < N_EXPERTS, _dot_hl(h2, wr_ref[...]) + br_ref[...], -jnp.inf)
    vals, ids = [], []
    for _ in range(TOP_K):
        m = jnp.max(logits, axis=-1, keepdims=True)
        am = jnp.min(jnp.where(logits == m, lanef, float(LANES)), axis=-1, keepdims=True)
        vals.append(m)
        ids.append(am)
        logits = jnp.where(lanef == am, -jnp.inf, logits)
    es = [jnp.exp(v - vals[0]) for v in vals]
    den = es[0] + es[1] + es[2] + es[3]

    ohs = [lanef == ids[k] for k in range(TOP_K)]
    cnt = jnp.zeros((TR, LANES), F32)
    for oh in ohs:
        cnt = cnt + jnp.where(oh, 1.0, 0.0)
    strict = jnp.where(_iota((TR, TR), 1) < _iota((TR, TR), 0), 1.0, 0.0).astype(BF16)
    before = jnp.dot(strict, cnt.astype(BF16), preferred_element_type=F32)
    tile_cnt = jnp.sum(cnt, axis=0, keepdims=True)
    tile_cnt = jnp.floor((tile_cnt + float(RUN_ALIGN - 1)) * (1.0 / RUN_ALIGN)) * float(RUN_ALIGN)
    below = jnp.where(_iota((LANES, LANES), 0) < _iota((LANES, LANES), 1), 1.0, 0.0).astype(BF16)
    lstart = _dot_sel_r(jnp.broadcast_to(tile_cnt, (SUBLANES, LANES)), below)[0:1, :]
    slot_base = before + lstart
    n_loc = LOC_SLOTS
    slot_iota = _iota((TR, n_loc), 1).astype(F32)
    place = jnp.zeros((TR, n_loc), F32)
    lp_o = jnp.zeros((TR, LANES), F32)
    wts_o = jnp.zeros((TR, LANES), F32)
    for k in range(TOP_K):
        lp = jnp.sum(jnp.where(ohs[k], slot_base, 0.0), axis=-1, keepdims=True)
        place = place + jnp.where(slot_iota == lp, 1.0, 0.0)
        lp_o = jnp.where(lane == k, lp, lp_o)
        wts_o = jnp.where(lane == k, es[k] / den, wts_o)
    lp_ref[...] = lp_o.astype(I32)
    wts_ref[...] = wts_o
    cnt_ref[...] = jnp.broadcast_to(tile_cnt, (SUBLANES, LANES)).astype(I32)

    hs = _dot_tn(place, h2)
    w_hi = pltpu.bitcast(hs[:, :d // 2], U32) & jnp.uint32(0xFFFF0000)
    w_lo = pltpu.bitcast(hs[:, d // 2:], U32) >> 16
    hs_ref[...] = w_hi | w_lo


def _merge(xa, mod_l, g1, g2, att_o, o_f, o_b, gate, conf_u, wg, wao, wdo, wco, wout, dng, wr, br):
    b, t, d = xa.shape
    nt = t // TR
    row = lambda bi, ti: (bi, ti, 0)
    const = lambda bi, ti: (0, 0)
    full = lambda a: pl.BlockSpec(a.shape, const)
    rowspec = lambda w: pl.BlockSpec((None, TR, w), row)
    return pl.pallas_call(
        _merge_kernel,
        grid=(b, nt),
        in_specs=[rowspec(d),
                  pl.BlockSpec((None, N_MOD, d), lambda bi, ti: (jnp.where(ti == 0, b, bi), 0, 0)),
                  full(g1), full(g2), rowspec(ATT_HQ * ATT_DH), rowspec(DN_H * DN_DV), rowspec(DN_H * DN_DV),
                  rowspec(DN_H * DN_DV), rowspec(CONF_CH),
                  full(wg), full(wao), full(wdo), full(wco), full(wout), full(dng), full(wr), full(br)],
        out_specs=[rowspec(d),
                   pl.BlockSpec((None, LOC_SLOTS, d // 2), row),
                   rowspec(LANES), rowspec(LANES),
                   pl.BlockSpec((None, None, SUBLANES, LANES), lambda bi, ti: (bi, ti, 0, 0))],
        out_shape=[jax.ShapeDtypeStruct((b, t, d), F32),
                   jax.ShapeDtypeStruct((b, nt * LOC_SLOTS, d // 2), U32),
                   jax.ShapeDtypeStruct((b, t, LANES), I32),
                   jax.ShapeDtypeStruct((b, t, LANES), F32),
                   jax.ShapeDtypeStruct((b, nt, SUBLANES, LANES), I32)],
        compiler_params=_cparams(("parallel", "parallel"), VMEM_LIMIT),
        name="merge_router",
    )(xa, mod_l, g1, g2, att_o, o_f, o_b, gate, conf_u, wg, wao, wdo, wco, wout, dng, wr, br)


def _offsets_kernel(cnt_ref, dst_ref, meta_ref, *, n_tiles_pad):
    cnt = cnt_ref[...].astype(F32)
    nt = cnt.shape[0]
    earlier = jnp.where(_iota((nt, nt), 1) < _iota((nt, nt), 0), 1.0, 0.0).astype(BF16)
    carry = _dot_sel_l(jnp.concatenate([earlier] * 3, axis=1), cnt)
    tot = jnp.broadcast_to(jnp.sum(cnt, axis=0, keepdims=True), (SUBLANES, LANES))
    padded = jnp.floor((tot + float(TM - 1)) * (1.0 / TM)) * float(TM)
    incl = jnp.where(_iota((LANES, LANES), 0) <= _iota((LANES, LANES), 1), 1.0, 0.0).astype(BF16)
    end = _dot_sel_r(padded, incl)
    start = (end - padded)[0:1, :]
    dst_ref[...] = (carry + start).astype(I32)

    end_t = jnp.transpose(jnp.broadcast_to(end[0:1, :], (LANES, LANES)))
    end_w = jnp.concatenate([end_t] * (n_tiles_pad // LANES), axis=1)
    tile0 = _iota((LANES, n_tiles_pad), 1).astype(F32) * float(TM)
    valid_e = _iota((LANES, n_tiles_pad), 0) < N_EXPERTS
    te = jnp.sum(jnp.where(valid_e & (end_w <= tile0), 1.0, 0.0), axis=0, keepdims=True)
    te = jnp.minimum(te, float(N_EXPERTS - 1))
    n_used = end[0:1, N_EXPERTS - 1:N_EXPERTS] * (1.0 / TM)
    gend = jnp.concatenate([end[0:1, :]] + [jnp.zeros((1, n_tiles_pad - LANES), F32)], axis=1)
    r8 = _iota((SUBLANES, n_tiles_pad), 0)
    meta = jnp.where(r8 == 0, te, jnp.where(r8 == 1, gend, jnp.where(r8 == 2, n_used, 0.0)))
    meta_ref[...] = meta.astype(I32)


def _offsets(cnt_tiles, n_tiles_pad):
    ntp = cnt_tiles.shape[0]
    return pl.pallas_call(
        functools.partial(_offsets_kernel, n_tiles_pad=n_tiles_pad),
        grid=(1,),
        in_specs=[pl.BlockSpec((ntp, LANES), lambda i: (0, 0))],
        out_specs=[pl.BlockSpec((ntp, LANES), lambda i: (0, 0)),
                   pl.BlockSpec((SUBLANES, n_tiles_pad), lambda i: (0, 0))],
        out_shape=[jax.ShapeDtypeStruct((ntp, LANES), I32),
                   jax.ShapeDtypeStruct((SUBLANES, n_tiles_pad), I32)],
        compiler_params=_cparams(("arbitrary",)),
        name="moe_offsets",
    )(cnt_tiles)


def _run_copies(cnt_ref, dst_ref, tile, local_ref, slots_ref, sem, to_slots, action):
    def per_expert(e, ls):
        cnt = cnt_ref[tile * LANES + e]
        dst = dst_ref[tile * LANES + e]
        for bit in range(RUN_BITS):
            size = RUN_ALIGN << bit
            off = cnt & (size - 1)

            @pl.when((cnt & size) != 0)
            def _():
                loc = local_ref.at[pl.ds(pl.multiple_of(ls + off, RUN_ALIGN), size)]
                slot = slots_ref.at[pl.ds(pl.multiple_of(dst + off, RUN_ALIGN), size)]
                action(pltpu.make_async_copy(loc, slot, sem) if to_slots
                       else pltpu.make_async_copy(slot, loc, sem))
        return ls + cnt

    lax.fori_loop(0, N_EXPERTS, per_expert, 0)


def _dispatch_kernel(gend_ref, cnt_ref, dst_ref, hs_ref, xs_ref, zero_ref, sem, zsem):
    def zero_copy(e):
        ge = gend_ref[e]
        return pltpu.make_async_copy(zero_ref, xs_ref.at[pl.ds(pl.multiple_of(ge - TM, TM), TM)], zsem)

    def nonempty(e):
        prev = gend_ref[jnp.maximum(e - 1, 0)]
        return jnp.where(e == 0, gend_ref[0] > 0, gend_ref[e] > prev)

    @pl.when(pl.program_id(0) == 0)
    def _():
        zero_ref[...] = jnp.zeros_like(zero_ref)

        def start(e, c):
            @pl.when(nonempty(e))
            def _():
                zero_copy(e).start()
            return c

        def wait(e, c):
            @pl.when(nonempty(e))
            def _():
                zero_copy(e).wait()
            return c

        lax.fori_loop(0, N_EXPERTS, start, 0)
        lax.fori_loop(0, N_EXPERTS, wait, 0)

        def tail_copy(i):
            return pltpu.make_async_copy(zero_ref, xs_ref.at[pl.ds(pl.multiple_of(i * TM, TM), TM)], zsem)

        def tail_start(i, c):
            tail_copy(i).start()
            return c

        def tail_wait(i, c):
            tail_copy(i).wait()
            return c

        n_used = gend_ref[N_EXPERTS - 1] // TM
        lax.fori_loop(n_used, xs_ref.shape[0] // TM, tail_start, 0)
        lax.fori_loop(n_used, xs_ref.shape[0] // TM, tail_wait, 0)

    tile = pl.program_id(0)
    _run_copies(cnt_ref, dst_ref, tile, hs_ref, xs_ref, sem, True, lambda cp: cp.start())
    _run_copies(cnt_ref, dst_ref, tile, hs_ref, xs_ref, sem, True, lambda cp: cp.wait())


def _dispatch(gend, cnt_flat, dst_flat, hs2, n_slots):
    n, w = hs2.shape
    rows = LOC_SLOTS
    grid_spec = pltpu.PrefetchScalarGridSpec(
        num_scalar_prefetch=3,
        grid=(n // rows,),
        in_specs=[pl.BlockSpec((rows, w), lambda i, g, c_, d_: (i, 0))],
        out_specs=pl.BlockSpec(memory_space=pl.ANY),
        scratch_shapes=[pltpu.VMEM((TM, w), U32), pltpu.SemaphoreType.DMA, pltpu.SemaphoreType.DMA],
    )
    return pl.pallas_call(
        _dispatch_kernel,
        grid_spec=grid_spec,
        out_shape=jax.ShapeDtypeStruct((n_slots, w), U32),
        compiler_params=_cparams(("arbitrary",)),
        name="moe_dispatch",
    )(gend, cnt_flat, dst_flat, hs2)


def _expert_kernel(te_ref, nu_ref, xs_ref, wgu_ref, bgu_ref, wdn_ref, bdn_ref, ys_ref, wgu_b, wdn_b):
    i = pl.program_id(0)
    prev = te_ref[jnp.maximum(i - 1, 0)]
    new_expert = (i == 0) | (te_ref[i] != prev)

    @pl.when(i < nu_ref[0])
    def _():
        @pl.when(new_expert)
        def _():
            wgu_b[...] = wgu_ref[...].astype(BF16)
            wdn_b[...] = wdn_ref[...].astype(BF16)

        wd = xs_ref[...]
        x_hi = pltpu.bitcast(wd & jnp.uint32(0xFFFF0000), F32).astype(BF16)
        x_lo = pltpu.bitcast(wd << 16, F32).astype(BF16)
        x = jnp.concatenate([x_hi, x_lo], axis=1)
        gu = jnp.dot(x, wgu_b[...], preferred_element_type=F32) + bgu_ref[...]
        gate = jnp.minimum(gu[:, :D_FF], SWIGLU_LIMIT)
        up = jnp.clip(gu[:, D_FF:], -SWIGLU_LIMIT, SWIGLU_LIMIT)
        act = (up + 1.0) * (gate * _sigmoid(SWIGLU_ALPHA * gate))
        ys_ref[...] = jnp.dot(act.astype(BF16), wdn_b[...], preferred_element_type=F32) + bdn_ref[...]

    @pl.when(i >= nu_ref[0])
    def _():
        ys_ref[...] = jnp.zeros_like(ys_ref)


def _experts(te, nu, xs, layer, wgu, bgu, wdn, bdn):
    n_slots, w = xs.shape
    n_l, n_e, d, ff2 = wgu.shape
    used = lambda i, te_, nu_: jnp.minimum(i, nu_[0] - 1)
    grid_spec = pltpu.PrefetchScalarGridSpec(
        num_scalar_prefetch=2,
        grid=(n_slots // TM,),
        in_specs=[pl.BlockSpec((TM, w), lambda i, te_, nu_: (used(i, te_, nu_), 0)),
                  pl.BlockSpec((None, None, d, ff2), lambda i, te_, nu_: (layer, te_[i], 0, 0)),
                  pl.BlockSpec((None, None, 1, ff2), lambda i, te_, nu_: (layer, te_[i], 0, 0)),
                  pl.BlockSpec((None, None, ff2 // 2, d), lambda i, te_, nu_: (layer, te_[i], 0, 0)),
                  pl.BlockSpec((None, None, 1, d), lambda i, te_, nu_: (layer, te_[i], 0, 0))],
        out_specs=pl.BlockSpec((TM, d), lambda i, te_, nu_: (i, 0)),
        scratch_shapes=[pltpu.VMEM((d, ff2), BF16), pltpu.VMEM((ff2 // 2, d), BF16)],
    )
    return pl.pallas_call(
        _expert_kernel,
        grid_spec=grid_spec,
        out_shape=jax.ShapeDtypeStruct((n_slots, d), F32),
        compiler_params=_cparams(("arbitrary",), VMEM_LIMIT),
        name="moe_experts",
    )(te, nu, xs, wgu, bgu.reshape(n_l, n_e, 1, ff2), wdn, bdn.reshape(n_l, n_e, 1, d))


def _combine_kernel(cnt_ref, dst_ref, x_ref, mod_ref, wts_ref, lp_ref, ys_ref, o_ref, buf_ref, sem):
    tile = pl.program_id(0) * pl.num_programs(1) + pl.program_id(1)
    buf_ref[TR * TOP_K:, :] = jnp.zeros((LOC_SLOTS - TR * TOP_K, buf_ref.shape[1]), F32)
    _run_copies(cnt_ref, dst_ref, tile, buf_ref, ys_ref, sem, False, lambda cp: cp.start())
    _run_copies(cnt_ref, dst_ref, tile, buf_ref, ys_ref, sem, False, lambda cp: cp.wait())
    wts = wts_ref[...]
    lp = lp_ref[...].astype(F32)
    slot_iota = _iota((TR, LOC_SLOTS), 1).astype(F32)
    pw = jnp.zeros((TR, LOC_SLOTS), F32)
    for k in range(TOP_K):
        pw = pw + jnp.where(slot_iota == lp[:, k:k + 1], wts[:, k:k + 1], 0.0)
    o_ref[...] = x_ref[...] + mod_ref[5:6, :] * _dot_hl(pw, buf_ref[...])


def _combine(cnt_flat, dst_flat, xa, mod_l, wts, lp, ys):
    b, t, d = xa.shape
    nt = t // TR
    row = lambda bi, ti, c_, d_: (bi, ti, 0)
    grid_spec = pltpu.PrefetchScalarGridSpec(
        num_scalar_prefetch=2,
        grid=(b, nt),
        in_specs=[pl.BlockSpec((None, TR, d), row),
                  pl.BlockSpec((None, N_MOD, d), lambda bi, ti, c_, d_: (jnp.where(ti == 0, b, bi), 0, 0)),
                  pl.BlockSpec((None, TR, LANES), row),
                  pl.BlockSpec((None, TR, LANES), row),
                  pl.BlockSpec(memory_space=pl.ANY)],
        out_specs=pl.BlockSpec((None, TR, d), row),
        scratch_shapes=[pltpu.VMEM((LOC_SLOTS, d), F32), pltpu.SemaphoreType.DMA],
    )
    return pl.pallas_call(
        _combine_kernel,
        grid_spec=grid_spec,
        out_shape=jax.ShapeDtypeStruct((b, t, d), F32),
        compiler_params=_cparams(("arbitrary", "arbitrary"), VMEM_LIMIT),
        name="moe_combine",
    )(cnt_flat, dst_flat, xa, mod_l, wts, lp, ys)


def _final_kernel(x_ref, g_ref, o_ref):
    x = x_ref[...]
    o_ref[...] = x * lax.rsqrt(jnp.mean(x * x, axis=-1, keepdims=True) + EPS) * g_ref[...]


def _final_norm(xa, g, n_ctx):
    b, t, d = xa.shape
    skip = n_ctx // TR
    return pl.pallas_call(
        _final_kernel,
        grid=(b, (t - n_ctx) // TR),
        in_specs=[pl.BlockSpec((None, TR, d), lambda bi, ti: (bi, ti + skip, 0)),
                  pl.BlockSpec((1, d), lambda bi, ti: (0, 0))],
        out_specs=pl.BlockSpec((None, TR, d), lambda bi, ti: (bi, ti, 0)),
        out_shape=jax.ShapeDtypeStruct((b, t - n_ctx, d), F32),
        compiler_params=_cparams(("parallel", "parallel")),
        name="final_norm",
    )(xa, g.reshape(1, d))


def _rope_tables(n_ctx, seq):
    rows = seq // GRID_W
    r, col = jnp.meshgrid(jnp.arange(rows, dtype=F32), jnp.arange(GRID_W, dtype=F32), indexing='ij')
    inv = ROPE_THETA ** (-jnp.arange(ROPE_PAIRS, dtype=F32) / ROPE_PAIRS)
    ang_r = r.reshape(-1)[:, None] * inv
    ang_c = col.reshape(-1)[:, None] * inv
    cos_h = jnp.concatenate([jnp.cos(ang_r)] * 2 + [jnp.cos(ang_c)] * 2, axis=1)
    sin_h = jnp.concatenate([-jnp.sin(ang_r), jnp.sin(ang_r), -jnp.sin(ang_c), jnp.sin(ang_c)], axis=1)
    cos_t = jnp.concatenate([jnp.ones((n_ctx, ATT_DH), F32), cos_h], axis=0)
    sin_t = jnp.concatenate([jnp.zeros((n_ctx, ATT_DH), F32), sin_h], axis=0)
    reps = LANES // ATT_DH
    return jnp.tile(cos_t, (1, reps)), jnp.tile(sin_t, (1, reps))


def _small_weight(w_in_l):
    d = w_in_l.shape[0]
    off_beta = ATT_W + DN_W
    off_g = off_beta + 2 * DN_H
    cols = []
    for h in range(DN_H):
        src = [off_beta + h, off_beta + DN_H + h, off_g + h, off_g + DN_H + h]
        cols.append(jnp.concatenate([w_in_l[:, s:s + 1] for s in src]
                                    + [jnp.zeros((d, LANES - 4), w_in_l.dtype)], axis=1))
    return jnp.concatenate(cols, axis=1)


def _small_vec(v2):
    cols = []
    for h in range(DN_H):
        cols.append(jnp.concatenate([jnp.zeros((2,), F32), v2[0:1, h], v2[1:2, h],
                                     jnp.zeros((LANES - 4,), F32)]))
    return jnp.concatenate(cols).reshape(1, SM_W)


def kernel(x, c, ctx, c_ctx, w_mod, b_mod, norm1_g, w_in, q_norm_g, k_norm_g, dn_conv_w, dn_a_log,
           dn_dt_bias, dn_norm_g, conf_dw_w, conf_dw_b, conf_ln_g, conf_ln_b, w_att_o, w_dn_o, w_conf_o,
           w_out, norm2_g, w_router, b_router, w_gate_up, b_gate_up, w_down, b_down, final_g):
    b, seq, d = x.shape
    n_ctx = ctx.shape[1]
    depth = w_mod.shape[0]
    assert d == D_MODEL and n_ctx == TR and seq % TR == 0 and seq % GRID_W == 0
    t = n_ctx + seq
    n_tok = b * t
    n_slots = -(-((n_tok // TR) * LOC_SLOTS) // TM) * TM + N_EXPERTS * TM
    n_tiles_pad = -(-(n_slots // TM) // LANES) * LANES

    xa = jnp.concatenate([ctx, x], axis=1)
    mod_rows = -(-(b + 1) // SUBLANES) * SUBLANES
    cc = jnp.concatenate([c, c_ctx[None, :], jnp.zeros((mod_rows - b - 1, d), F32)], axis=0)
    mod = _mod_all(cc, w_mod, b_mod).reshape(depth, mod_rows, N_MOD, d)

    cos_t, sin_t = _rope_tables(n_ctx, seq)
    hblk = _iota((LANES, LANES), 0) // ATT_DH == _iota((LANES, LANES), 1) // ATT_DH
    bavg = jnp.where(hblk, 1.0 / ATT_DH, 0.0).astype(BF16)

    o_dn = ATT_W
    o_gate = ATT_W + DN_W + 4 * DN_H
    o_conf = o_gate + DN_H * DN_DV
    o_br = o_conf + 2 * CONF_CH

    for l in range(depth):
        w = w_in[l]
        mod_l = mod[l]
        g1 = norm1_g[l].reshape(1, d)
        qkg = jnp.concatenate([jnp.tile(q_norm_g[l], ATT_HQ), jnp.tile(k_norm_g[l], ATT_HKV)]).reshape(1, ATT_QK)
        att, dnqkv, sm, gate, glu = _inproj(
            xa, mod_l, g1, w[:, :ATT_W].astype(BF16), w[:, o_dn:o_dn + DN_W].astype(BF16),
            _small_weight(w).astype(BF16), w[:, o_gate:o_conf].astype(BF16),
            w[:, o_conf:o_br].astype(BF16), qkg, cos_t, sin_t, bavg,
            _small_vec(dn_a_log[l]), _small_vec(dn_dt_bias[l]))

        att_o = _attention(att)
        u, wq, kd, intra, egl = _dnprep(dnqkv, dn_conv_w[l], sm, n_ctx)
        o_f, o_b = _dnscan(u, wq, kd, intra, egl)
        conf_u = _conformer(glu, conf_dw_w[l], conf_dw_b[l].reshape(1, CONF_CH),
                            conf_ln_g[l].reshape(1, CONF_CH), conf_ln_b[l].reshape(1, CONF_CH), n_ctx)

        wr = jnp.concatenate([w_router[l], jnp.zeros((d, LANES - N_EXPERTS), F32)], axis=1)
        br = jnp.concatenate([b_router[l], jnp.zeros((LANES - N_EXPERTS,), F32)]).reshape(1, LANES)
        x1, hs, lp, wts, cnt = _merge(
            xa, mod_l, g1, norm2_g[l].reshape(1, d), att_o, o_f, o_b, gate, conf_u,
            w[:, o_br:].astype(BF16), w_att_o[l].astype(BF16), w_dn_o[l].astype(BF16),
            w_conf_o[l].astype(BF16), w_out[l].astype(BF16), dn_norm_g[l].reshape(1, DN_DV), wr, br)

        n_tt = n_tok // TR
        n_tt_pad = -(-n_tt // LANES) * LANES
        cnt_tiles = jnp.concatenate([cnt[:, :, 0, :].reshape(n_tt, LANES),
                                     jnp.zeros((n_tt_pad - n_tt, LANES), I32)], axis=0)
        dst, meta = _offsets(cnt_tiles, n_tiles_pad)
        te = meta[0, :n_slots // TM]
        gend = meta[1, :N_EXPERTS]
        nu = meta[2, :1]
        cnt_flat = cnt_tiles.reshape(-1)
        dst_flat = dst.reshape(-1)
        xs = _dispatch(gend, cnt_flat, dst_flat, hs.reshape((n_tok // TR) * LOC_SLOTS, d // 2), n_slots)
        ys = _experts(te, nu, xs, l, w_gate_up, b_gate_up, w_down, b_down)
        xa = _combine(cnt_flat, dst_flat, x1, mod_l, wts, lp, ys)

    return _final_norm(xa, final_g, n_ctx)
```

```python
import functools

import jax
import jax.numpy as jnp
from jax import lax
from jax.experimental import pallas as pl
from jax.experimental.pallas import tpu as pltpu

F32 = jnp.float32
BF16 = jnp.bfloat16
I32 = jnp.int32
U32 = jnp.uint32

D_MODEL = 1024
N_MOD = 6
EPS = 1e-6
GRID_W = 64
ATT_HQ, ATT_HKV, ATT_DH = 8, 2, 64
ATT_GROUP = ATT_HQ // ATT_HKV
ROPE_THETA = 10000.0
ROPE_PAIRS = ATT_DH // 4
DN_H, DN_DK, DN_DV, DN_CONV, DN_CHUNK = 4, 128, 128, 4, 64
CONF_CH, CONF_KW = 512, 31
N_EXPERTS, TOP_K, D_FF = 32, 4, 1024
SWIGLU_LIMIT, SWIGLU_ALPHA = 7.0, 1.702
N_BRANCH = 3

LANES = 128
SUBLANES = 8
TR = 256
TM = 512
DN_GROUP = TR // DN_CHUNK
SCAN_STATE_BATCH = 4
DN_PREP_TILES = 3
RUN_ALIGN = SUBLANES
RUN_BITS = (TR // RUN_ALIGN).bit_length()
LOC_SLOTS = TR * TOP_K + N_EXPERTS * RUN_ALIGN
CONF_TR = 128
CONF_HALO = 16
ATT_W = ATT_HQ * ATT_DH + 2 * ATT_HKV * ATT_DH
ATT_QK = ATT_HQ * ATT_DH + ATT_HKV * ATT_DH
DN_W = 3 * DN_H * DN_DK
SM_W = DN_H * LANES
VMEM_LIMIT = 52 * 1024 * 1024


def _cparams(sem, vmem=None):
    return pltpu.CompilerParams(dimension_semantics=sem, vmem_limit_bytes=vmem)


def _dot(a, b):
    return jnp.dot(a.astype(BF16), b.astype(BF16), preferred_element_type=F32)


def _dot_nt(a, b):
    return lax.dot_general(a.astype(BF16), b.astype(BF16), (((1,), (1,)), ((), ())),
                           preferred_element_type=F32)


def _dot_tn(a, b):
    return lax.dot_general(a.astype(BF16), b.astype(BF16), (((0,), (0,)), ((), ())),
                           preferred_element_type=F32)


def _split2(x):
    hi = x.astype(BF16)
    lo = (x - hi.astype(F32)).astype(BF16)
    return hi, lo


def _split3(x):
    hi = x.astype(BF16)
    r = x - hi.astype(F32)
    mid = r.astype(BF16)
    lo = (r - mid.astype(F32)).astype(BF16)
    return hi, mid, lo


def _dot_sel_l(sel3, x):
    return jnp.dot(sel3, jnp.concatenate(_split3(x), axis=0), preferred_element_type=F32)


def _dot_sel_r(x, sel):
    return jnp.dot(jnp.concatenate(_split3(x), axis=1), jnp.concatenate([sel] * 3, axis=0),
                   preferred_element_type=F32)


def _dot_hl(a, b):
    ah, al = _split2(a)
    bh, bl = _split2(b)
    return jnp.dot(jnp.concatenate([ah, ah, al], axis=1), jnp.concatenate([bh, bl, bh], axis=0),
                   preferred_element_type=F32)


def _sigmoid(x):
    return jax.nn.sigmoid(x)


def _silu(x):
    return x * jax.nn.sigmoid(x)


def _softplus(x):
    return jnp.maximum(x, 0.0) + jnp.log1p(jnp.exp(-jnp.abs(x)))


def _rms_mod(x, g, shift, scale):
    ms = jnp.mean(x * x, axis=-1, keepdims=True)
    y = x * lax.rsqrt(ms + EPS) * g
    return y * (1.0 + scale) + shift


def _iota(shape, dim):
    return lax.broadcasted_iota(I32, shape, dim)


def _mod_kernel(cc_ref, w_ref, b_ref, o_ref):
    o_ref[...] = _dot_hl(_silu(cc_ref[...]), w_ref[...]) + b_ref[...]


def _mod_all(cc, w_mod, b_mod):
    n_l, d, n = w_mod.shape
    tn = 1536
    rows = cc.shape[0]
    return pl.pallas_call(
        _mod_kernel,
        grid=(n_l, n // tn),
        in_specs=[pl.BlockSpec((rows, d), lambda l, j: (0, 0)),
                  pl.BlockSpec((None, d, tn), lambda l, j: (l, 0, j)),
                  pl.BlockSpec((None, 1, tn), lambda l, j: (l, 0, j))],
        out_specs=pl.BlockSpec((None, rows, tn), lambda l, j: (l, 0, j)),
        out_shape=jax.ShapeDtypeStruct((n_l, rows, n), F32),
        compiler_params=_cparams(("parallel", "parallel"), VMEM_LIMIT),
        name="mod_all",
    )(cc, w_mod, b_mod.reshape(n_l, 1, n))


def _inproj_kernel(x_ref, mod_ref, g1_ref, watt_ref, wdn_ref, wsm_ref, wgate_ref, wconf_ref,
                   qkg_ref, cos_ref, sin_ref, bavg_ref, alog_ref, dtb_ref,
                   att_ref, dnqkv_ref, sm_ref, gate_ref, glu_ref):
    h = _rms_mod(x_ref[...], g1_ref[...], mod_ref[0:1, :], mod_ref[1:2, :]).astype(BF16)

    za = jnp.dot(h, watt_ref[...], preferred_element_type=F32)
    cs = cos_ref[...]
    sn = sin_ref[...]
    bavg = bavg_ref[...]
    first = (_iota((TR, LANES), 1) & 31) < ROPE_PAIRS
    n_q_blocks = ATT_HQ * ATT_DH // LANES
    for j in range(ATT_QK // LANES):
        sl = slice(LANES * j, LANES * (j + 1))
        blk = za[:, sl]
        hi, lo = _split2(blk * blk)
        ms = jnp.dot(hi, bavg, preferred_element_type=F32) + jnp.dot(lo, bavg, preferred_element_type=F32)
        y = blk * lax.rsqrt(ms + EPS) * qkg_ref[:, sl]
        rot = jnp.where(first, pltpu.roll(y, LANES - ROPE_PAIRS, 1), pltpu.roll(y, ROPE_PAIRS, 1))
        y = y * cs + rot * sn
        if j < n_q_blocks:
            y = y * (ATT_DH ** -0.5)
        att_ref[:, sl] = y.astype(BF16)
    att_ref[:, ATT_QK:ATT_W] = za[:, ATT_QK:ATT_W].astype(BF16)

    dnqkv_ref[...] = jnp.dot(h, wdn_ref[...], preferred_element_type=F32)

    zs = jnp.dot(h, wsm_ref[...], preferred_element_type=F32)
    ln = _iota(zs.shape, 1) & (LANES - 1)
    g = -jnp.exp(alog_ref[...]) * _softplus(zs + dtb_ref[...])
    sm_ref[...] = jnp.where(ln < 2, _sigmoid(zs), jnp.where(ln < 4, g, 0.0))

    gate_ref[...] = jnp.dot(h, wgate_ref[...], preferred_element_type=F32)

    zc = jnp.dot(h, wconf_ref[...], preferred_element_type=F32)
    glu_ref[...] = zc[:, :CONF_CH] * _sigmoid(zc[:, CONF_CH:])


def _inproj(xa, mod_l, g1, watt, wdn, wsm, wgate, wconf, qkg, cos_t, sin_t, bavg, alog, dtb):
    b, t, d = xa.shape
    nt = t // TR
    row = lambda bi, ti: (bi, ti, 0)
    const = lambda bi, ti: (0, 0)
    full = lambda a: pl.BlockSpec(a.shape, const)
    return pl.pallas_call(
        _inproj_kernel,
        grid=(b, nt),
        in_specs=[pl.BlockSpec((None, TR, d), row),
                  pl.BlockSpec((None, N_MOD, d), lambda bi, ti: (jnp.where(ti == 0, b, bi), 0, 0)),
                  full(g1), full(watt), full(wdn), full(wsm), full(wgate), full(wconf), full(qkg),
                  pl.BlockSpec((TR, LANES), lambda bi, ti: (ti, 0)),
                  pl.BlockSpec((TR, LANES), lambda bi, ti: (ti, 0)),
                  full(bavg), full(alog), full(dtb)],
        out_specs=[pl.BlockSpec((None, TR, ATT_W), row),
                   pl.BlockSpec((None, TR, DN_W), row),
                   pl.BlockSpec((None, TR, SM_W), row),
                   pl.BlockSpec((None, TR, DN_H * DN_DV), row),
                   pl.BlockSpec((None, TR, CONF_CH), row)],
        out_shape=[jax.ShapeDtypeStruct((b, t, ATT_W), BF16),
                   jax.ShapeDtypeStruct((b, t, DN_W), F32),
                   jax.ShapeDtypeStruct((b, t, SM_W), F32),
                   jax.ShapeDtypeStruct((b, t, DN_H * DN_DV), F32),
                   jax.ShapeDtypeStruct((b, t, CONF_CH), F32)],
        compiler_params=_cparams(("parallel", "parallel"), VMEM_LIMIT),
        name="inproj",
    )(xa, mod_l, g1, watt, wdn, wsm, wgate, wconf, qkg, cos_t, sin_t, bavg, alog, dtb)


def _attn_kernel(q_ref, kv_ref, o_ref, *, n_keys):
    def run(nk):
        for kvh in range(ATT_HKV):
            k = kv_ref[0:nk, ATT_HQ * ATT_DH + ATT_DH * kvh:ATT_HQ * ATT_DH + ATT_DH * (kvh + 1)]
            v = kv_ref[0:nk, ATT_QK + ATT_DH * kvh:ATT_QK + ATT_DH * (kvh + 1)]
            for gi in range(ATT_GROUP):
                hq = kvh * ATT_GROUP + gi
                q = q_ref[:, ATT_DH * hq:ATT_DH * (hq + 1)]
                s = lax.dot_general(q, k, (((1,), (1,)), ((), ())), preferred_element_type=F32)
                p = jnp.exp(s - jnp.max(s, axis=-1, keepdims=True))
                l = jnp.sum(p, axis=-1, keepdims=True)
                o = jnp.dot(p.astype(BF16), v, preferred_element_type=F32) / l
                o_ref[:, ATT_DH * hq:ATT_DH * (hq + 1)] = o.astype(BF16)

    @pl.when(pl.program_id(1) == 0)
    def _():
        run(TR)

    @pl.when(pl.program_id(1) != 0)
    def _():
        run(n_keys)


def _attention(att):
    b, t, _ = att.shape
    nt = t // TR
    return pl.pallas_call(
        functools.partial(_attn_kernel, n_keys=t),
        grid=(b, nt),
        in_specs=[pl.BlockSpec((None, TR, ATT_W), lambda bi, ti: (bi, ti, 0)),
                  pl.BlockSpec((None, t, ATT_W), lambda bi, ti: (bi, 0, 0))],
        out_specs=pl.BlockSpec((None, TR, ATT_HQ * ATT_DH), lambda bi, ti: (bi, ti, 0)),
        out_shape=jax.ShapeDtypeStruct((b, t, ATT_HQ * ATT_DH), BF16),
        compiler_params=_cparams(("parallel", "parallel"), VMEM_LIMIT),
        name="attention",
    )(att, att)


def _dnprep_kernel(q_ref, k_ref, v_ref, cwq_ref, cwk_ref, cwv_ref, sm_ref,
                   u_ref, wq_ref, kd_ref, in_ref, egl_ref,
                   qs_ref, ks_ref, vs_ref, *, n_ctx):
    t = q_ref.shape[0]
    row = _iota((t, LANES), 0)
    seg_first = (row == 0) | (row == n_ctx)
    seg_last = (row == n_ctx - 1) | (row == t - 1)
    seg_last2 = seg_last | (row == n_ctx - 2) | (row == t - 2)

    def conv_silu(x_ref, w_ref_):
        x = x_ref[...]
        xm1 = jnp.where(seg_first, 0.0, pltpu.roll(x, 1, 0))
        xp1 = jnp.where(seg_last, 0.0, pltpu.roll(x, t - 1, 0))
        xp2 = jnp.where(seg_last2, 0.0, pltpu.roll(x, t - 2, 0))
        y = w_ref_[0:1, :] * xm1 + w_ref_[1:2, :] * x + w_ref_[2:3, :] * xp1 + w_ref_[3:4, :] * xp2
        return _silu(y)

    def l2n(x):
        return x * lax.rsqrt(jnp.sum(x * x, axis=-1, keepdims=True) + EPS)

    qs_ref[...] = l2n(conv_silu(q_ref, cwq_ref)) * (DN_DK ** -0.5)
    ks_ref[...] = l2n(conv_silu(k_ref, cwk_ref))
    vs_ref[...] = conv_silu(v_ref, cwv_ref)

    c = DN_CHUNK
    wd = c * DN_GROUP
    ri = _iota((c, wd), 0)
    cj = _iota((c, wd), 1) & (c - 1)
    lower_w = ri >= cj
    upper_w = ri <= cj
    eye_w = jnp.where(ri == cj, 1.0, 0.0).astype(F32)
    same_blk = (_iota((wd, wd), 0) // c) == (_iota((wd, wd), 1) // c)
    blk_b = jnp.where(same_blk, 1.0, 0.0).astype(BF16)
    r3 = _iota((c, 3 * c), 0)
    c3 = _iota((c, 3 * c), 1) & (c - 1)
    sel_lower = jnp.where(r3 >= c3, 1.0, 0.0).astype(BF16)
    sel_upper = jnp.where(r3 <= c3, 1.0, 0.0).astype(BF16)
    ones_b = jnp.ones((c, 3 * c), BF16)

    def fold(x):
        xm = jnp.where(same_blk, x, 0.0)
        out = xm[0:c]
        for s in range(1, DN_GROUP):
            out = out + xm[s * c:(s + 1) * c]
        return out

    def bdiag(xw):
        return jnp.concatenate([xw] * DN_GROUP, axis=0) * blk_b

    def tall(xw):
        xt = jnp.where(same_blk, jnp.concatenate([xw] * DN_GROUP, axis=0), -jnp.inf)
        return jnp.max(xt, axis=-1, keepdims=True)

    def wdot(a_hl, b_hl):
        (ah, al), (bh, bl) = a_hl, b_hl
        bdh = bdiag(bh)
        return jnp.dot(jnp.concatenate([ah, ah, al], axis=1),
                       jnp.concatenate([bdh, bdiag(bl), bdh], axis=0), preferred_element_type=F32)

    def process(tiles):
        jobs = []
        for gi in tiles:
            r0 = gi * wd if isinstance(gi, int) else pl.multiple_of(gi * wd, wd)
            q = qs_ref[pl.ds(r0, wd), :]
            k = ks_ref[pl.ds(r0, wd), :]
            v = vs_ref[pl.ds(r0, wd), :]
            smt = sm_ref[pl.ds(r0, wd), :]
            qk = fold(_dot_nt(q, k))
            for d in (0, 1):
                incl = lower_w if d == 0 else upper_w
                incl_t = upper_w if d == 0 else lower_w
                strict = (ri > cj) if d == 0 else (ri < cj)
                beta = smt[:, d:d + 1]
                g = smt[:, 2 + d:3 + d]
                gw = fold(jnp.broadcast_to(g, (wd, wd)))
                col = _dot_sel_l(sel_lower if d == 0 else sel_upper, gw)
                rowm = _dot_sel_l(ones_b, jnp.where(incl_t, gw, 0.0))
                decay = jnp.exp(jnp.where(incl, col - rowm, -jnp.inf))
                gc = tall(col)
                gl = tall(jnp.broadcast_to(col[c - 1:c, :] if d == 0 else col[0:1, :], (c, wd)))
                kb = k * beta
                lw = jnp.where(strict, fold(_dot_nt(kb, k)) * decay, 0.0)
                jobs.append(dict(gi=gi, r0=r0, d=d, q=q, k=k, vb=v * beta, kb=kb, gc=gc, gl=gl,
                                 lw=lw, intra=jnp.where(incl, qk * decay, 0.0)))

        tms = [eye_w - j["lw"] for j in jobs]
        pws = [_split2(j["lw"]) for j in jobs]
        for _ in range(5):
            pws = [_split2(wdot(pw, pw)) for pw in pws]
            tms = [tm + wdot(_split2(tm), pw) for tm, pw in zip(tms, pws)]

        for j, tm in zip(jobs, tms):
            gi, r0, d = j["gi"], j["r0"], j["d"]
            eg = jnp.exp(j["gc"])
            uw = jnp.dot(bdiag(tm.astype(BF16)),
                         jnp.concatenate([j["vb"], j["kb"] * eg], axis=1).astype(BF16),
                         preferred_element_type=F32)
            u_ref[d, pl.ds(r0, wd), :] = uw[:, :LANES]
            wv = uw[:, LANES:].astype(BF16)
            qg = (j["q"] * eg).astype(BF16)
            egl = jnp.exp(j["gl"])
            for s in range(DN_GROUP):
                r2 = 2 * r0 + 2 * c * s
                e0 = (gi * DN_GROUP + s) * SUBLANES
                if not isinstance(gi, int):
                    r2 = pl.multiple_of(r2, 2 * c)
                    e0 = pl.multiple_of(e0, SUBLANES)
                wq_ref[d, pl.ds(r2, c), :] = wv[s * c:(s + 1) * c]
                wq_ref[d, pl.ds(r2 + c, c), :] = qg[s * c:(s + 1) * c]
                egl_ref[d, pl.ds(e0, SUBLANES), :] = jnp.broadcast_to(egl[s * c:s * c + 1], (SUBLANES, LANES))
            kd_ref[d, gi] = jnp.transpose(j["k"] * jnp.exp(j["gl"] - j["gc"])).astype(BF16)
            i0 = gi * c if isinstance(gi, int) else pl.multiple_of(gi * c, c)
            in_ref[d, pl.ds(i0, c), :] = j["intra"].astype(BF16)

    n_tiles = t // wd

    def batch(bi, carry):
        process([DN_PREP_TILES * bi + s for s in range(DN_PREP_TILES)])
        return carry

    lax.fori_loop(0, n_tiles // DN_PREP_TILES, batch, 0)
    rest = list(range(n_tiles - n_tiles % DN_PREP_TILES, n_tiles))
    if rest:
        process(rest)


def _dnprep(dnqkv, conv_w, sm, n_ctx):
    b, t, _ = dnqkv.shape
    nc = t // DN_CHUNK
    col = lambda off: pl.BlockSpec((None, t, LANES), lambda bi, hi: (bi, 0, off + hi))
    wcol = lambda off: pl.BlockSpec((DN_CONV, LANES), lambda bi, hi: (0, off + hi))
    big = lambda rows: pl.BlockSpec((None, None, 2, rows, LANES), lambda bi, hi: (bi, hi, 0, 0, 0))
    return pl.pallas_call(
        functools.partial(_dnprep_kernel, n_ctx=n_ctx),
        grid=(b, DN_H),
        in_specs=[col(0), col(DN_H), col(2 * DN_H), wcol(0), wcol(DN_H), wcol(2 * DN_H),
                  pl.BlockSpec((None, t, LANES), lambda bi, hi: (bi, 0, hi))],
        out_specs=[big(t), big(2 * t),
                   pl.BlockSpec((None, None, 2, t // TR, DN_DK, TR), lambda bi, hi: (bi, hi, 0, 0, 0, 0)),
                   pl.BlockSpec((None, None, 2, t // DN_GROUP, TR), lambda bi, hi: (bi, hi, 0, 0, 0)),
                   pl.BlockSpec((None, None, 2, nc * SUBLANES, LANES), lambda bi, hi: (bi, hi, 0, 0, 0))],
        out_shape=[jax.ShapeDtypeStruct((b, DN_H, 2, t, LANES), F32),
                   jax.ShapeDtypeStruct((b, DN_H, 2, 2 * t, LANES), BF16),
                   jax.ShapeDtypeStruct((b, DN_H, 2, t // TR, DN_DK, TR), BF16),
                   jax.ShapeDtypeStruct((b, DN_H, 2, t // DN_GROUP, TR), BF16),
                   jax.ShapeDtypeStruct((b, DN_H, 2, nc * SUBLANES, LANES), F32)],
        scratch_shapes=[pltpu.VMEM((t, LANES), F32)] * 3,
        compiler_params=_cparams(("parallel", "parallel"), VMEM_LIMIT),
        name="dn_prep",
    )(dnqkv, dnqkv, dnqkv, conv_w, conv_w, conv_w, sm)


def _dnscan_kernel(uf_ref, wqf_ref, kdf_ref, inf_ref, egf_ref,
                   ub_ref, wqb_ref, kdb_ref, inb_ref, egb_ref,
                   of_ref, ob_ref, s_ref):
    @pl.when(pl.program_id(1) == 0)
    def _():
        s_ref[...] = jnp.zeros_like(s_ref)

    c = DN_CHUNK
    n_cc = TR // c
    fwd = (uf_ref, wqf_ref, kdf_ref, inf_ref, egf_ref, of_ref)
    bwd = (ub_ref, wqb_ref, kdb_ref, inb_ref, egb_ref, ob_ref)

    def bdiag(blocks):
        rows = []
        for i, blk in enumerate(blocks):
            z = jnp.zeros_like(blk)
            rows.append(jnp.concatenate([blk if j == i else z for j in range(len(blocks))], axis=1))
        return jnp.concatenate(rows, axis=0)

    chains = [(h, d) for h in range(DN_H) for d in (0, 1)]
    states = [s_ref[2 * h + d] for h, d in chains]
    for cc in range(n_cc):
        chunk_of = (cc, n_cc - 1 - cc)
        ws = []
        for g0 in range(0, len(chains), SCAN_STATE_BATCH):
            grp = range(g0, g0 + SCAN_STATE_BATCH)
            lhs = jnp.concatenate(
                [(fwd if chains[i][1] == 0 else bwd)[1][
                    chains[i][0], 2 * c * chunk_of[chains[i][1]]:2 * c * (chunk_of[chains[i][1]] + 1), :]
                 for i in grp], axis=1)
            res = jnp.dot(lhs, bdiag([states[i].astype(BF16) for i in grp]), preferred_element_type=F32)
            ws += [res[:, DN_DV * j:DN_DV * (j + 1)] for j in range(SCAN_STATE_BATCH)]
        for g0 in range(0, len(chains), 2):
            lhs_parts, rhs_parts = [], []
            for i in (g0, g0 + 1):
                h, d = chains[i]
                u_r, _, kd_r, in_r, _, _ = fwd if d == 0 else bwd
                ch = chunk_of[d]
                vn = (u_r[h, ch * c:(ch + 1) * c, :] - ws[i][:c]).astype(BF16)
                pad = [jnp.zeros((c, DN_DV), BF16)] * n_cc
                pad[ch] = vn
                rhs_parts.append(jnp.concatenate(pad, axis=0))
                lhs_parts.append(jnp.concatenate([in_r[h], kd_r[h]], axis=0))
            res = jnp.dot(jnp.concatenate(lhs_parts, axis=1), bdiag(rhs_parts), preferred_element_type=F32)
            for j, i in enumerate((g0, g0 + 1)):
                h, d = chains[i]
                _, _, _, _, eg_r, o_r = fwd if d == 0 else bwd
                ch = chunk_of[d]
                o_r[ch * c:(ch + 1) * c, LANES * h:LANES * (h + 1)] = (
                    ws[i][c:] + res[:c, DN_DV * j:DN_DV * (j + 1)])
                states[i] = (states[i] * eg_r[h, SUBLANES * ch:SUBLANES * ch + 1, :]
                             + res[c:, DN_DV * j:DN_DV * (j + 1)])
    for i, (h, d) in enumerate(chains):
        s_ref[2 * h + d] = states[i]


def _dnscan(u, wq, kd, intra, egl):
    b, _, _, t, _ = u.shape
    nt = t // TR
    epr = (TR // DN_CHUNK) * SUBLANES
    tb = lambda ti: jnp.where(ti == 0, 0, nt - ti)
    big_f = lambda rows: pl.BlockSpec((None, DN_H, None, rows, LANES), lambda bi, ti: (bi, 0, 0, ti, 0))
    big_b = lambda rows: pl.BlockSpec((None, DN_H, None, rows, LANES), lambda bi, ti: (bi, 0, 1, tb(ti), 0))
    kd_f = pl.BlockSpec((None, DN_H, None, None, DN_DK, TR), lambda bi, ti: (bi, 0, 0, ti, 0, 0))
    kd_b = pl.BlockSpec((None, DN_H, None, None, DN_DK, TR), lambda bi, ti: (bi, 0, 1, tb(ti), 0, 0))
    in_f = pl.BlockSpec((None, DN_H, None, DN_CHUNK, TR), lambda bi, ti: (bi, 0, 0, ti, 0))
    in_b = pl.BlockSpec((None, DN_H, None, DN_CHUNK, TR), lambda bi, ti: (bi, 0, 1, tb(ti), 0))
    width = DN_H * DN_DV
    return pl.pallas_call(
        _dnscan_kernel,
        grid=(b, nt),
        in_specs=[big_f(TR), big_f(2 * TR), kd_f, in_f, big_f(epr),
                  big_b(TR), big_b(2 * TR), kd_b, in_b, big_b(epr)],
        out_specs=[pl.BlockSpec((None, TR, width), lambda bi, ti: (bi, ti, 0)),
                   pl.BlockSpec((None, TR, width), lambda bi, ti: (bi, tb(ti), 0))],
        out_shape=[jax.ShapeDtypeStruct((b, t, width), F32)] * 2,
        scratch_shapes=[pltpu.VMEM((2 * DN_H, DN_DK, DN_DV), F32)],
        compiler_params=_cparams(("parallel", "arbitrary"), VMEM_LIMIT),
        name="dn_scan",
    )(u, wq, kd, intra, egl, u, wq, kd, intra, egl)


def _conf_kernel(glu_ref, dww_ref, dwb_ref, lng_ref, lnb_ref, o_ref, pad_ref, u_ref, *, n_ctx):
    t = glu_ref.shape[0]
    halo = CONF_HALO
    lat0 = halo + n_ctx + 2 * halo
    zeros = lambda n: jnp.zeros((n, CONF_CH), F32)
    pad_ref[0:halo, :] = zeros(halo)
    pad_ref[halo + n_ctx:lat0, :] = zeros(2 * halo)
    pad_ref[lat0 + (t - n_ctx):lat0 + (t - n_ctx) + halo, :] = zeros(halo)
    pad_ref[halo:halo + n_ctx, :] = glu_ref[0:n_ctx, :]
    pad_ref[lat0:lat0 + (t - n_ctx), :] = glu_ref[n_ctx:t, :]

    n_ctx_tiles = n_ctx // CONF_TR
    blk_rows = CONF_TR + 2 * halo
    center = (CONF_KW - 1) // 2

    def conv_tile(i, carry):
        t0 = pl.multiple_of(i * CONF_TR, CONF_TR)
        p0 = pl.multiple_of(t0 + jnp.where(i >= n_ctx_tiles, 2 * halo, 0), SUBLANES)
        for lb in range(CONF_CH // LANES):
            sl = slice(LANES * lb, LANES * (lb + 1))
            blk = pad_ref[pl.ds(p0, blk_rows), sl]
            acc = jnp.zeros((CONF_TR, LANES), F32)
            for r in range(SUBLANES):
                rolled = blk if r == 0 else pltpu.roll(blk, blk_rows - r, 0)
                for j in range(CONF_KW):
                    off = halo + j - center
                    if off % SUBLANES != r:
                        continue
                    a8 = off - r
                    acc = acc + dww_ref[j:j + 1, sl] * rolled[a8:a8 + CONF_TR, :]
            u_ref[pl.ds(t0, CONF_TR), sl] = acc + dwb_ref[:, sl]
        return carry

    lax.fori_loop(0, t // CONF_TR, conv_tile, 0)

    def ln_tile(i, carry):
        t0 = pl.multiple_of(i * CONF_TR, CONF_TR)
        x = u_ref[pl.ds(t0, CONF_TR), :]
        mu = jnp.mean(x, axis=-1, keepdims=True)
        xc = x - mu
        var = jnp.mean(xc * xc, axis=-1, keepdims=True)
        y = xc * lax.rsqrt(var + EPS) * lng_ref[...] + lnb_ref[...]
        o_ref[pl.ds(t0, CONF_TR), :] = _silu(y).astype(BF16)
        return carry

    lax.fori_loop(0, t // CONF_TR, ln_tile, 0)


def _conformer(glu, dww, dwb, lng, lnb, n_ctx):
    b, t, ch = glu.shape
    pad_rows = t + 4 * CONF_HALO
    const = lambda bi: (0, 0)
    return pl.pallas_call(
        functools.partial(_conf_kernel, n_ctx=n_ctx),
        grid=(b,),
        in_specs=[pl.BlockSpec((None, t, ch), lambda bi: (bi, 0, 0)),
                  pl.BlockSpec(dww.shape, const), pl.BlockSpec(dwb.shape, const),
                  pl.BlockSpec(lng.shape, const), pl.BlockSpec(lnb.shape, const)],
        out_specs=pl.BlockSpec((None, t, ch), lambda bi: (bi, 0, 0)),
        out_shape=jax.ShapeDtypeStruct((b, t, ch), BF16),
        scratch_shapes=[pltpu.VMEM((pad_rows, ch), F32), pltpu.VMEM((t, ch), F32)],
        compiler_params=_cparams(("parallel",), VMEM_LIMIT),
        name="conformer",
    )(glu, dww, dwb, lng, lnb)


def _merge_kernel(x_ref, mod_ref, g1_ref, g2_ref, att_ref, of_ref, ob_ref, gate_ref, cu_ref,
                  wg_ref, wao_ref, wdo_ref, wco_ref, wout_ref, dng_ref, wr_ref, br_ref,
                  xo_ref, hs_ref, lp_ref, wts_ref, cnt_ref):
    x = x_ref[...]
    h = _rms_mod(x, g1_ref[...], mod_ref[0:1, :], mod_ref[1:2, :]).astype(BF16)

    o = of_ref[...] + ob_ref[...]
    gate = gate_ref[...]
    dn_parts = []
    for hd in range(DN_H):
        sl = slice(LANES * hd, LANES * (hd + 1))
        oh = o[:, sl]
        oh = oh * lax.rsqrt(jnp.mean(oh * oh, axis=-1, keepdims=True) + EPS) * dng_ref[...]
        dn_parts.append((oh * _silu(gate[:, sl])).astype(BF16))
    dn_o = jnp.concatenate(dn_parts, axis=1)

    y_att = jnp.dot(att_ref[...], wao_ref[...], preferred_element_type=F32)
    y_dn = jnp.dot(dn_o, wdo_ref[...], preferred_element_type=F32)
    y_cf = jnp.dot(cu_ref[...], wco_ref[...], preferred_element_type=F32)
    d = D_MODEL
    y = (_sigmoid(jnp.dot(h, wg_ref[:, 0:d], preferred_element_type=F32)) * y_att
         + _sigmoid(jnp.dot(h, wg_ref[:, d:2 * d], preferred_element_type=F32)) * y_dn
         + _sigmoid(jnp.dot(h, wg_ref[:, 2 * d:3 * d], preferred_element_type=F32)) * y_cf)
    x1 = x + mod_ref[2:3, :] * jnp.dot(y.astype(BF16), wout_ref[...], preferred_element_type=F32)
    xo_ref[...] = x1

    h2 = _rms_mod(x1, g2_ref[...], mod_ref[3:4, :], mod_ref[4:5, :])

    lane = _iota((TR, LANES), 1)
    lanef = lane.astype(F32)
    logits = jnp.where(lane < N_EXPERTS, _dot_hl(h2, wr_ref[...]) + br_ref[...], -jnp.inf)
    vals, ids = [], []
    for _ in range(TOP_K):
        m = jnp.max(logits, axis=-1, keepdims=True)
        am = jnp.min(jnp.where(logits == m, lanef, float(LANES)), axis=-1, keepdims=True)
        vals.append(m)
        ids.append(am)
        logits = jnp.where(lanef == am, -jnp.inf, logits)
    es = [jnp.exp(v - vals[0]) for v in vals]
    den = es[0] + es[1] + es[2] + es[3]

    ohs = [lanef == ids[k] for k in range(TOP_K)]
    cnt = jnp.zeros((TR, LANES), F32)
    for oh in ohs:
        cnt = cnt + jnp.where(oh, 1.0, 0.0)
    strict = jnp.where(_iota((TR, TR), 1) < _iota((TR, TR), 0), 1.0, 0.0).astype(BF16)
    before = jnp.dot(strict, cnt.astype(BF16), preferred_element_type=F32)
    tile_cnt = jnp.sum(cnt, axis=0, keepdims=True)
    tile_cnt = jnp.floor((tile_cnt + float(RUN_ALIGN - 1)) * (1.0 / RUN_ALIGN)) * float(RUN_ALIGN)
    below = jnp.where(_iota((LANES, LANES), 0) < _iota((LANES, LANES), 1), 1.0, 0.0).astype(BF16)
    lstart = _dot_sel_r(jnp.broadcast_to(tile_cnt, (SUBLANES, LANES)), below)[0:1, :]
    slot_base = before + lstart
    n_loc = LOC_SLOTS
    slot_iota = _iota((TR, n_loc), 1).astype(F32)
    place = jnp.zeros((TR, n_loc), F32)
    lp_o = jnp.zeros((TR, LANES), F32)
    wts_o = jnp.zeros((TR, LANES), F32)
    for k in range(TOP_K):
        lp = jnp.sum(jnp.where(ohs[k], slot_base, 0.0), axis=-1, keepdims=True)
        place = place + jnp.where(slot_iota == lp, 1.0, 0.0)
        lp_o = jnp.where(lane == k, lp, lp_o)
        wts_o = jnp.where(lane == k, es[k] / den, wts_o)
    lp_ref[...] = lp_o.astype(I32)
    wts_ref[...] = wts_o
    cnt_ref[...] = jnp.broadcast_to(tile_cnt, (SUBLANES, LANES)).astype(I32)

    hs = _dot_tn(place, h2)
    w_hi = pltpu.bitcast(hs[:, :d // 2], U32) & jnp.uint32(0xFFFF0000)
    w_lo = pltpu.bitcast(hs[:, d // 2:], U32) >> 16
    hs_ref[...] = w_hi | w_lo


def _merge(xa, mod_l, g1, g2, att_o, o_f, o_b, gate, conf_u, wg, wao, wdo, wco, wout, dng, wr, br):
    b, t, d = xa.shape
    nt = t // TR
    row = lambda bi, ti: (bi, ti, 0)
    const = lambda bi, ti: (0, 0)
    full = lambda a: pl.BlockSpec(a.shape, const)
    rowspec = lambda w: pl.BlockSpec((None, TR, w), row)
    return pl.pallas_call(
        _merge_kernel,
        grid=(b, nt),
        in_specs=[rowspec(d),
                  pl.BlockSpec((None, N_MOD, d), lambda bi, ti: (jnp.where(ti == 0, b, bi), 0, 0)),
                  full(g1), full(g2), rowspec(ATT_HQ * ATT_DH), rowspec(DN_H * DN_DV), rowspec(DN_H * DN_DV),
                  rowspec(DN_H * DN_DV), rowspec(CONF_CH),
                  full(wg), full(wao), full(wdo), full(wco), full(wout), full(dng), full(wr), full(br)],
        out_specs=[rowspec(d),
                   pl.BlockSpec((None, LOC_SLOTS, d // 2), row),
                   rowspec(LANES), rowspec(LANES),
                   pl.BlockSpec((None, None, SUBLANES, LANES), lambda bi, ti: (bi, ti, 0, 0))],
        out_shape=[jax.ShapeDtypeStruct((b, t, d), F32),
                   jax.ShapeDtypeStruct((b, nt * LOC_SLOTS, d // 2), U32),
                   jax.ShapeDtypeStruct((b, t, LANES), I32),
                   jax.ShapeDtypeStruct((b, t, LANES), F32),
                   jax.ShapeDtypeStruct((b, nt, SUBLANES, LANES), I32)],
        compiler_params=_cparams(("parallel", "parallel"), VMEM_LIMIT),
        name="merge_router",
    )(xa, mod_l, g1, g2, att_o, o_f, o_b, gate, conf_u, wg, wao, wdo, wco, wout, dng, wr, br)


def _offsets_kernel(cnt_ref, dst_ref, meta_ref, *, n_tiles_pad):
    cnt = cnt_ref[...].astype(F32)
    nt = cnt.shape[0]
    earlier = jnp.where(_iota((nt, nt), 1) < _iota((nt, nt), 0), 1.0, 0.0).astype(BF16)
    carry = _dot_sel_l(jnp.concatenate([earlier] * 3, axis=1), cnt)
    tot = jnp.broadcast_to(jnp.sum(cnt, axis=0, keepdims=True), (SUBLANES, LANES))
    padded = jnp.floor((tot + float(TM - 1)) * (1.0 / TM)) * float(TM)
    incl = jnp.where(_iota((LANES, LANES), 0) <= _iota((LANES, LANES), 1), 1.0, 0.0).astype(BF16)
    end = _dot_sel_r(padded, incl)
    start = (end - padded)[0:1, :]
    dst_ref[...] = (carry + start).astype(I32)

    end_t = jnp.transpose(jnp.broadcast_to(end[0:1, :], (LANES, LANES)))
    end_w = jnp.concatenate([end_t] * (n_tiles_pad // LANES), axis=1)
    tile0 = _iota((LANES, n_tiles_pad), 1).astype(F32) * float(TM)
    valid_e = _iota((LANES, n_tiles_pad), 0) < N_EXPERTS
    te = jnp.sum(jnp.where(valid_e & (end_w <= tile0), 1.0, 0.0), axis=0, keepdims=True)
    te = jnp.minimum(te, float(N_EXPERTS - 1))
    n_used = end[0:1, N_EXPERTS - 1:N_EXPERTS] * (1.0 / TM)
    gend = jnp.concatenate([end[0:1, :]] + [jnp.zeros((1, n_tiles_pad - LANES), F32)], axis=1)
    r8 = _iota((SUBLANES, n_tiles_pad), 0)
    meta = jnp.where(r8 == 0, te, jnp.where(r8 == 1, gend, jnp.where(r8 == 2, n_used, 0.0)))
    meta_ref[...] = meta.astype(I32)


def _offsets(cnt_tiles, n_tiles_pad):
    ntp = cnt_tiles.shape[0]
    return pl.pallas_call(
        functools.partial(_offsets_kernel, n_tiles_pad=n_tiles_pad),
        grid=(1,),
        in_specs=[pl.BlockSpec((ntp, LANES), lambda i: (0, 0))],
        out_specs=[pl.BlockSpec((ntp, LANES), lambda i: (0, 0)),
                   pl.BlockSpec((SUBLANES, n_tiles_pad), lambda i: (0, 0))],
        out_shape=[jax.ShapeDtypeStruct((ntp, LANES), I32),
                   jax.ShapeDtypeStruct((SUBLANES, n_tiles_pad), I32)],
        compiler_params=_cparams(("arbitrary",)),
        name="moe_offsets",
    )(cnt_tiles)


def _run_copies(cnt_ref, dst_ref, tile, local_ref, slots_ref, sem, to_slots, action):
    def per_expert(e, ls):
        cnt = cnt_ref[tile * LANES + e]
        dst = dst_ref[tile * LANES + e]
        for bit in range(RUN_BITS):
            size = RUN_ALIGN << bit
            off = cnt & (size - 1)

            @pl.when((cnt & size) != 0)
            def _():
                loc = local_ref.at[pl.ds(pl.multiple_of(ls + off, RUN_ALIGN), size)]
                slot = slots_ref.at[pl.ds(pl.multiple_of(dst + off, RUN_ALIGN), size)]
                action(pltpu.make_async_copy(loc, slot, sem) if to_slots
                       else pltpu.make_async_copy(slot, loc, sem))
        return ls + cnt

    lax.fori_loop(0, N_EXPERTS, per_expert, 0)


def _dispatch_kernel(gend_ref, cnt_ref, dst_ref, hs_ref, xs_ref, zero_ref, sem, zsem):
    def zero_copy(e):
        ge = gend_ref[e]
        return pltpu.make_async_copy(zero_ref, xs_ref.at[pl.ds(pl.multiple_of(ge - TM, TM), TM)], zsem)

    def nonempty(e):
        prev = gend_ref[jnp.maximum(e - 1, 0)]
        return jnp.where(e == 0, gend_ref[0] > 0, gend_ref[e] > prev)

    @pl.when(pl.program_id(0) == 0)
    def _():
        zero_ref[...] = jnp.zeros_like(zero_ref)

        def start(e, c):
            @pl.when(nonempty(e))
            def _():
                zero_copy(e).start()
            return c

        def wait(e, c):
            @pl.when(nonempty(e))
            def _():
                zero_copy(e).wait()
            return c

        lax.fori_loop(0, N_EXPERTS, start, 0)
        lax.fori_loop(0, N_EXPERTS, wait, 0)

        def tail_copy(i):
            return pltpu.make_async_copy(zero_ref, xs_ref.at[pl.ds(pl.multiple_of(i * TM, TM), TM)], zsem)

        def tail_start(i, c):
            tail_copy(i).start()
            return c

        def tail_wait(i, c):
            tail_copy(i).wait()
            return c

        n_used = gend_ref[N_EXPERTS - 1] // TM
        lax.fori_loop(n_used, xs_ref.shape[0] // TM, tail_start, 0)
        lax.fori_loop(n_used, xs_ref.shape[0] // TM, tail_wait, 0)

    tile = pl.program_id(0)
    _run_copies(cnt_ref, dst_ref, tile, hs_ref, xs_ref, sem, True, lambda cp: cp.start())
    _run_copies(cnt_ref, dst_ref, tile, hs_ref, xs_ref, sem, True, lambda cp: cp.wait())


def _dispatch(gend, cnt_flat, dst_flat, hs2, n_slots):
    n, w = hs2.shape
    rows = LOC_SLOTS
    grid_spec = pltpu.PrefetchScalarGridSpec(
        num_scalar_prefetch=3,
        grid=(n // rows,),
        in_specs=[pl.BlockSpec((rows, w), lambda i, g, c_, d_: (i, 0))],
        out_specs=pl.BlockSpec(memory_space=pl.ANY),
        scratch_shapes=[pltpu.VMEM((TM, w), U32), pltpu.SemaphoreType.DMA, pltpu.SemaphoreType.DMA],
    )
    return pl.pallas_call(
        _dispatch_kernel,
        grid_spec=grid_spec,
        out_shape=jax.ShapeDtypeStruct((n_slots, w), U32),
        compiler_params=_cparams(("arbitrary",)),
        name="moe_dispatch",
    )(gend, cnt_flat, dst_flat, hs2)


def _expert_kernel(te_ref, nu_ref, xs_ref, wgu_ref, bgu_ref, wdn_ref, bdn_ref, ys_ref, wgu_b, wdn_b):
    i = pl.program_id(0)
    prev = te_ref[jnp.maximum(i - 1, 0)]
    new_expert = (i == 0) | (te_ref[i] != prev)

    @pl.when(i < nu_ref[0])
    def _():
        @pl.when(new_expert)
        def _():
            wgu_b[...] = wgu_ref[...].astype(BF16)
            wdn_b[...] = wdn_ref[...].astype(BF16)

        wd = xs_ref[...]
        x_hi = pltpu.bitcast(wd & jnp.uint32(0xFFFF0000), F32).astype(BF16)
        x_lo = pltpu.bitcast(wd << 16, F32).astype(BF16)
        x = jnp.concatenate([x_hi, x_lo], axis=1)
        gu = jnp.dot(x, wgu_b[...], preferred_element_type=F32) + bgu_ref[...]
        gate = jnp.minimum(gu[:, :D_FF], SWIGLU_LIMIT)
        up = jnp.clip(gu[:, D_FF:], -SWIGLU_LIMIT, SWIGLU_LIMIT)
        act = (up + 1.0) * (gate * _sigmoid(SWIGLU_ALPHA * gate))
        ys_ref[...] = jnp.dot(act.astype(BF16), wdn_b[...], preferred_element_type=F32) + bdn_ref[...]

    @pl.when(i >= nu_ref[0])
    def _():
        ys_ref[...] = jnp.zeros_like(ys_ref)


def _experts(te, nu, xs, layer, wgu, bgu, wdn, bdn):
    n_slots, w = xs.shape
    n_l, n_e, d, ff2 = wgu.shape
    used = lambda i, te_, nu_: jnp.minimum(i, nu_[0] - 1)
    grid_spec = pltpu.PrefetchScalarGridSpec(
        num_scalar_prefetch=2,
        grid=(n_slots // TM,),
        in_specs=[pl.BlockSpec((TM, w), lambda i, te_, nu_: (used(i, te_, nu_), 0)),
                  pl.BlockSpec((None, None, d, ff2), lambda i, te_, nu_: (layer, te_[i], 0, 0)),
                  pl.BlockSpec((None, None, 1, ff2), lambda i, te_, nu_: (layer, te_[i], 0, 0)),
                  pl.BlockSpec((None, None, ff2 // 2, d), lambda i, te_, nu_: (layer, te_[i], 0, 0)),
                  pl.BlockSpec((None, None, 1, d), lambda i, te_, nu_: (layer, te_[i], 0, 0))],
        out_specs=pl.BlockSpec((TM, d), lambda i, te_, nu_: (i, 0)),
        scratch_shapes=[pltpu.VMEM((d, ff2), BF16), pltpu.VMEM((ff2 // 2, d), BF16)],
    )
    return pl.pallas_call(
        _expert_kernel,
        grid_spec=grid_spec,
        out_shape=jax.ShapeDtypeStruct((n_slots, d), F32),
        compiler_params=_cparams(("arbitrary",), VMEM_LIMIT),
        name="moe_experts",
    )(te, nu, xs, wgu, bgu.reshape(n_l, n_e, 1, ff2), wdn, bdn.reshape(n_l, n_e, 1, d))


def _combine_kernel(cnt_ref, dst_ref, x_ref, mod_ref, wts_ref, lp_ref, ys_ref, o_ref, buf_ref, sem):
    tile = pl.program_id(0) * pl.num_programs(1) + pl.program_id(1)
    buf_ref[TR * TOP_K:, :] = jnp.zeros((LOC_SLOTS - TR * TOP_K, buf_ref.shape[1]), F32)
    _run_copies(cnt_ref, dst_ref, tile, buf_ref, ys_ref, sem, False, lambda cp: cp.start())
    _run_copies(cnt_ref, dst_ref, tile, buf_ref, ys_ref, sem, False, lambda cp: cp.wait())
    wts = wts_ref[...]
    lp = lp_ref[...].astype(F32)
    slot_iota = _iota((TR, LOC_SLOTS), 1).astype(F32)
    pw = jnp.zeros((TR, LOC_SLOTS), F32)
    for k in range(TOP_K):
        pw = pw + jnp.where(slot_iota == lp[:, k:k + 1], wts[:, k:k + 1], 0.0)
    o_ref[...] = x_ref[...] + mod_ref[5:6, :] * _dot_hl(pw, buf_ref[...])


def _combine(cnt_flat, dst_flat, xa, mod_l, wts, lp, ys):
    b, t, d = xa.shape
    nt = t // TR
    row = lambda bi, ti, c_, d_: (bi, ti, 0)
    grid_spec = pltpu.PrefetchScalarGridSpec(
        num_scalar_prefetch=2,
        grid=(b, nt),
        in_specs=[pl.BlockSpec((None, TR, d), row),
                  pl.BlockSpec((None, N_MOD, d), lambda bi, ti, c_, d_: (jnp.where(ti == 0, b, bi), 0, 0)),
                  pl.BlockSpec((None, TR, LANES), row),
                  pl.BlockSpec((None, TR, LANES), row),
                  pl.BlockSpec(memory_space=pl.ANY)],
        out_specs=pl.BlockSpec((None, TR, d), row),
        scratch_shapes=[pltpu.VMEM((LOC_SLOTS, d), F32), pltpu.SemaphoreType.DMA],
    )
    return pl.pallas_call(
        _combine_kernel,
        grid_spec=grid_spec,
        out_shape=jax.ShapeDtypeStruct((b, t, d), F32),
        compiler_params=_cparams(("arbitrary", "arbitrary"), VMEM_LIMIT),
        name="moe_combine",
    )(cnt_flat, dst_flat, xa, mod_l, wts, lp, ys)


def _final_kernel(x_ref, g_ref, o_ref):
    x = x_ref[...]
    o_ref[...] = x * lax.rsqrt(jnp.mean(x * x, axis=-1, keepdims=True) + EPS) * g_ref[...]


def _final_norm(xa, g, n_ctx):
    b, t, d = xa.shape
    skip = n_ctx // TR
    return pl.pallas_call(
        _final_kernel,
        grid=(b, (t - n_ctx) // TR),
        in_specs=[pl.BlockSpec((None, TR, d), lambda bi, ti: (bi, ti + skip, 0)),
                  pl.BlockSpec((1, d), lambda bi, ti: (0, 0))],
        out_specs=pl.BlockSpec((None, TR, d), lambda bi, ti: (bi, ti, 0)),
        out_shape=jax.ShapeDtypeStruct((b, t - n_ctx, d), F32),
        compiler_params=_cparams(("parallel", "parallel")),
        name="final_norm",
    )(xa, g.reshape(1, d))


def _rope_tables(n_ctx, seq):
    rows = seq // GRID_W
    r, col = jnp.meshgrid(jnp.arange(rows, dtype=F32), jnp.arange(GRID_W, dtype=F32), indexing='ij')
    inv = ROPE_THETA ** (-jnp.arange(ROPE_PAIRS, dtype=F32) / ROPE_PAIRS)
    ang_r = r.reshape(-1)[:, None] * inv
    ang_c = col.reshape(-1)[:, None] * inv
    cos_h = jnp.concatenate([jnp.cos(ang_r)] * 2 + [jnp.cos(ang_c)] * 2, axis=1)
    sin_h = jnp.concatenate([-jnp.sin(ang_r), jnp.sin(ang_r), -jnp.sin(ang_c), jnp.sin(ang_c)], axis=1)
    cos_t = jnp.concatenate([jnp.ones((n_ctx, ATT_DH), F32), cos_h], axis=0)
    sin_t = jnp.concatenate([jnp.zeros((n_ctx, ATT_DH), F32), sin_h], axis=0)
    reps = LANES // ATT_DH
    return jnp.tile(cos_t, (1, reps)), jnp.tile(sin_t, (1, reps))


def _small_weight(w_in_l):
    d = w_in_l.shape[0]
    off_beta = ATT_W + DN_W
    off_g = off_beta + 2 * DN_H
    cols = []
    for h in range(DN_H):
        src = [off_beta + h, off_beta + DN_H + h, off_g + h, off_g + DN_H + h]
        cols.append(jnp.concatenate([w_in_l[:, s:s + 1] for s in src]
                                    + [jnp.zeros((d, LANES - 4), w_in_l.dtype)], axis=1))
    return jnp.concatenate(cols, axis=1)


def _small_vec(v2):
    cols = []
    for h in range(DN_H):
        cols.append(jnp.concatenate([jnp.zeros((2,), F32), v2[0:1, h], v2[1:2, h],
                                     jnp.zeros((LANES - 4,), F32)]))
    return jnp.concatenate(cols).reshape(1, SM_W)


def kernel(x, c, ctx, c_ctx, w_mod, b_mod, norm1_g, w_in, q_norm_g, k_norm_g, dn_conv_w, dn_a_log,
           dn_dt_bias, dn_norm_g, conf_dw_w, conf_dw_b, conf_ln_g, conf_ln_b, w_att_o, w_dn_o, w_conf_o,
           w_out, norm2_g, w_router, b_router, w_gate_up, b_gate_up, w_down, b_down, final_g):
    b, seq, d = x.shape
    n_ctx = ctx.shape[1]
    depth = w_mod.shape[0]
    assert d == D_MODEL and n_ctx == TR and seq % TR == 0 and seq % GRID_W == 0
    t = n_ctx + seq
    n_tok = b * t
    n_slots = -(-((n_tok // TR) * LOC_SLOTS) // TM) * TM + N_EXPERTS * TM
    n_tiles_pad = -(-(n_slots // TM) // LANES) * LANES

    xa = jnp.concatenate([ctx, x], axis=1)
    mod_rows = -(-(b + 1) // SUBLANES) * SUBLANES
    cc = jnp.concatenate([c, c_ctx[None, :], jnp.zeros((mod_rows - b - 1, d), F32)], axis=0)
    mod = _mod_all(cc, w_mod, b_mod).reshape(depth, mod_rows, N_MOD, d)

    cos_t, sin_t = _rope_tables(n_ctx, seq)
    hblk = _iota((LANES, LANES), 0) // ATT_DH == _iota((LANES, LANES), 1) // ATT_DH
    bavg = jnp.where(hblk, 1.0 / ATT_DH, 0.0).astype(BF16)

    o_dn = ATT_W
    o_gate = ATT_W + DN_W + 4 * DN_H
    o_conf = o_gate + DN_H * DN_DV
    o_br = o_conf + 2 * CONF_CH

    for l in range(depth):
        w = w_in[l]
        mod_l = mod[l]
        g1 = norm1_g[l].reshape(1, d)
        qkg = jnp.concatenate([jnp.tile(q_norm_g[l], ATT_HQ), jnp.tile(k_norm_g[l], ATT_HKV)]).reshape(1, ATT_QK)
        att, dnqkv, sm, gate, glu = _inproj(
            xa, mod_l, g1, w[:, :ATT_W].astype(BF16), w[:, o_dn:o_dn + DN_W].astype(BF16),
            _small_weight(w).astype(BF16), w[:, o_gate:o_conf].astype(BF16),
            w[:, o_conf:o_br].astype(BF16), qkg, cos_t, sin_t, bavg,
            _small_vec(dn_a_log[l]), _small_vec(dn_dt_bias[l]))

        att_o = _attention(att)
        u, wq, kd, intra, egl = _dnprep(dnqkv, dn_conv_w[l], sm, n_ctx)
        o_f, o_b = _dnscan(u, wq, kd, intra, egl)
        conf_u = _conformer(glu, conf_dw_w[l], conf_dw_b[l].reshape(1, CONF_CH),
                            conf_ln_g[l].reshape(1, CONF_CH), conf_ln_b[l].reshape(1, CONF_CH), n_ctx)

        wr = jnp.concatenate([w_router[l], jnp.zeros((d, LANES - N_EXPERTS), F32)], axis=1)
        br = jnp.concatenate([b_router[l], jnp.zeros((LANES - N_EXPERTS,), F32)]).reshape(1, LANES)
        x1, hs, lp, wts, cnt = _merge(
            xa, mod_l, g1, norm2_g[l].reshape(1, d), att_o, o_f, o_b, gate, conf_u,
            w[:, o_br:].astype(BF16), w_att_o[l].astype(BF16), w_dn_o[l].astype(BF16),
            w_conf_o[l].astype(BF16), w_out[l].astype(BF16), dn_norm_g[l].reshape(1, DN_DV), wr, br)

        n_tt = n_tok // TR
        n_tt_pad = -(-n_tt // LANES) * LANES
        cnt_tiles = jnp.concatenate([cnt[:, :, 0, :].reshape(n_tt, LANES),
                                     jnp.zeros((n_tt_pad - n_tt, LANES), I32)], axis=0)
        dst, meta = _offsets(cnt_tiles, n_tiles_pad)
        te = meta[0, :n_slots // TM]
        gend = meta[1, :N_EXPERTS]
        nu = meta[2, :1]
        cnt_flat = cnt_tiles.reshape(-1)
        dst_flat = dst.reshape(-1)
        xs = _dispatch(gend, cnt_flat, dst_flat, hs.reshape((n_tok // TR) * LOC_SLOTS, d // 2), n_slots)
        ys = _experts(te, nu, xs, l, w_gate_up, b_gate_up, w_down, b_down)
        xa = _combine(cnt_flat, dst_flat, x1, mod_l, wts, lp, ys)

    return _final_norm(xa, final_g, n_ctx)
```
